```python
import math
import jax
import jax.numpy as jnp
from jax import lax
import numpy as np

D_MODEL = 1024
BATCH = 2
SEQ = 8192
DEPTH = 2

EPS = 1e-6
GM_WIDTH = 512
GM_GROUPS = 4
GM_GROUP_DIM = GM_WIDTH // GM_GROUPS
GM_CHUNK = 128
SSM_INNER = 1024
SSM_HEAD_DIM = 64
SSM_HEADS = SSM_INNER // SSM_HEAD_DIM
SSM_GROUPS = 2
SSM_STATE = 128
SSM_CONV = 4
SSM_CONV_DIM = SSM_INNER + 2 * SSM_GROUPS * SSM_STATE
SSM_CHUNK = 128
DA_HEADS = 4
DA_HEAD_DIM = 64
DA_V_DIM = 2 * DA_HEAD_DIM
DA_QK_WIDTH = DA_HEADS * 2 * DA_HEAD_DIM
DA_WIDTH = DA_HEADS * DA_V_DIM
ROPE_DIM = DA_HEAD_DIM // 4
ROPE_THETA = 500000.0
Q_BLOCK = 128
N_BRANCH = 3
IN_SIZES = (GM_WIDTH, GM_WIDTH, SSM_INNER, SSM_CONV_DIM, SSM_HEADS, DA_QK_WIDTH, DA_QK_WIDTH, DA_WIDTH, N_BRANCH * D_MODEL)
IN_DIM = sum(IN_SIZES)
FFN_DIM = 2816
FFN_CONV = 3

kernel_name = 'hybrid_gmlp_ssd_diffattn_block'


def rmsnorm(x, g):
    xf = x.astype(jnp.float32)
    y = xf * lax.rsqrt(jnp.mean(xf * xf, axis=-1, keepdims=True) + EPS)
    return (y * g.astype(jnp.float32)).astype(x.dtype)


def causal_dwconv(x, w, b):
    k = w.shape[1]
    rhs = jnp.transpose(w)[:, None, :].astype(x.dtype)
    y = lax.conv_general_dilated(x, rhs, window_strides=(1,), padding=[(k - 1, 0)],
                                 dimension_numbers=('NWC', 'WIO', 'NWC'),
                                 feature_group_count=x.shape[-1])
    return y + b.astype(x.dtype)


def split_cols(t, sizes):
    idx = [int(i) for i in np.cumsum(sizes)[:-1]]
    return jnp.split(t, idx, axis=-1)


def chunked_spatial_gating(u, v, v_gain, w_s, b_s):
    bsz, s, _ = v.shape
    n = s // GM_CHUNK
    vn = rmsnorm(v, v_gain).reshape(bsz, n, GM_CHUNK, GM_GROUPS, GM_GROUP_DIM)
    causal = jnp.tril(jnp.ones((GM_CHUNK, GM_CHUNK), dtype=bool))
    w = jnp.where(causal[None], w_s, jnp.zeros_like(w_s))
    mixed = jnp.einsum('gts,bnsgc->bntgc', w, vn) + jnp.transpose(b_s)[None, None, :, :, None]
    return u * mixed.reshape(bsz, s, GM_WIDTH)


def ssd_scan(xs, dt, a_log, bm, cm, d_skip):
    bsz, s = xs.shape[:2]
    nc, q = s // SSM_CHUNK, SSM_CHUNK
    hpg = SSM_HEADS // SSM_GROUPS
    f32 = jnp.float32
    x = xs.astype(f32).reshape(bsz, nc, q, SSM_GROUPS, hpg, SSM_HEAD_DIM)
    dtc = dt.astype(f32).reshape(bsz, nc, q, SSM_GROUPS, hpg)
    bc = bm.astype(f32).reshape(bsz, nc, q, SSM_GROUPS, SSM_STATE)
    cc = cm.astype(f32).reshape(bsz, nc, q, SSM_GROUPS, SSM_STATE)
    a = -jnp.exp(a_log.astype(f32)).reshape(SSM_GROUPS, hpg)
    a_cum = jnp.cumsum(dtc * a, axis=2)
    xdt = x * dtc[..., None]
    causal = jnp.tril(jnp.ones((q, q), dtype=bool))[:, :, None, None]
    seg = a_cum[:, :, :, None] - a_cum[:, :, None, :]
    decay = jnp.exp(jnp.where(causal, seg, -jnp.inf))
    cb = jnp.einsum('bctgn,bcsgn->bctsg', cc, bc)
    y_diag = jnp.einsum('bctsgh,bcsghp->bctghp', cb[..., None] * decay, xdt)
    decay_end = jnp.exp(a_cum[:, :, -1:] - a_cum)
    states = jnp.einsum('bcsgn,bcsgh,bcsghp->bcghpn', bc, decay_end, xdt)
    chunk_decay = jnp.exp(a_cum[:, :, -1])

    def step(h, inp):
        st, dec = inp
        return h * dec[..., None, None] + st, h

    h0 = jnp.zeros((bsz, SSM_GROUPS, hpg, SSM_HEAD_DIM, SSM_STATE), f32)
    _, prev = lax.scan(step, h0, (jnp.moveaxis(states, 1, 0), jnp.moveaxis(chunk_decay, 1, 0)))
    prev = jnp.moveaxis(prev, 0, 1)
    y_off = jnp.einsum('bctgn,bcghpn,bctgh->bctghp', cc, prev, jnp.exp(a_cum))
    y = y_diag + y_off + d_skip.astype(f32).reshape(SSM_GROUPS, hpg)[..., None] * x
    return y.reshape(bsz, s, SSM_INNER).astype(xs.dtype)


def mamba2_mixer(z, xbc, dt_raw, conv_w, conv_b, dt_bias, a_log, d_skip, norm_g):
    bsz, s = z.shape[:2]
    xbc = jax.nn.silu(causal_dwconv(xbc, conv_w, conv_b))
    xs, bm, cm = split_cols(xbc, (SSM_INNER, SSM_GROUPS * SSM_STATE, SSM_GROUPS * SSM_STATE))
    dt = jax.nn.softplus(dt_raw.astype(jnp.float32) + dt_bias.astype(jnp.float32))
    y = ssd_scan(xs.reshape(bsz, s, SSM_HEADS, SSM_HEAD_DIM), dt, a_log,
                 bm.reshape(bsz, s, SSM_GROUPS, SSM_STATE),
                 cm.reshape(bsz, s, SSM_GROUPS, SSM_STATE), d_skip)
    return rmsnorm(y * jax.nn.silu(z), norm_g)


def rope_tables(s):
    pos = jnp.arange(s, dtype=jnp.float32)
    inv_freq = 1.0 / (ROPE_THETA ** (jnp.arange(0, ROPE_DIM, 2, dtype=jnp.float32) / ROPE_DIM))
    ang = pos[:, None] * inv_freq[None, :]
    return jnp.cos(ang), jnp.sin(ang)


def partial_rope(t, cos, sin):
    half = ROPE_DIM // 2
    c = cos[None, :, None, None, :]
    sn = sin[None, :, None, None, :]
    t1 = t[..., :half]
    t2 = t[..., half:ROPE_DIM]
    return jnp.concatenate([t1 * c - t2 * sn, t2 * c + t1 * sn, t[..., ROPE_DIM:]], axis=-1)


def diff_attention(q, k, v, q_g, k_g, lam_vecs, subln_g, lam_init, cos, sin):
    bsz, s = q.shape[:2]
    f32 = jnp.float32
    qh = rmsnorm(q.reshape(bsz, s, DA_HEADS, 2, DA_HEAD_DIM), q_g).astype(f32)
    kh = rmsnorm(k.reshape(bsz, s, DA_HEADS, 2, DA_HEAD_DIM), k_g).astype(f32)
    qh = partial_rope(qh, cos, sin) * (DA_HEAD_DIM ** -0.5)
    kh = partial_rope(kh, cos, sin)
    vf = v.reshape(bsz, s, DA_HEADS, DA_V_DIM).astype(f32)
    lv = lam_vecs.astype(f32)
    lam = jnp.exp(jnp.sum(lv[0] * lv[1])) - jnp.exp(jnp.sum(lv[2] * lv[3])) + lam_init
    nb = s // Q_BLOCK
    q_blocks = jnp.moveaxis(qh.reshape(bsz, nb, Q_BLOCK, DA_HEADS, 2, DA_HEAD_DIM), 1, 0)
    k_pos = jnp.arange(s)

    def block(args):
        qb, bi = args
        scores = jnp.einsum('bqhmd,bkhmd->bhmqk', qb, kh)
        q_pos = bi * Q_BLOCK + jnp.arange(Q_BLOCK)
        mask = k_pos[None, :] <= q_pos[:, None]
        p = jax.nn.softmax(jnp.where(mask, scores, -jnp.inf), axis=-1)
        w = p[:, :, 0] - lam * p[:, :, 1]
        return jnp.einsum('bhqk,bkhd->bqhd', w, vf)

    out = lax.map(block, (q_blocks, jnp.arange(nb)))
    out = jnp.moveaxis(out, 0, 1).reshape(bsz, s, DA_HEADS, DA_V_DIM)
    out = rmsnorm(out, subln_g) * (1.0 - lam_init)
    return out.reshape(bsz, s, DA_WIDTH).astype(v.dtype)


def setup_inputs(seed: int = 0) -> dict:
    key = jax.random.key(seed)
    ks = jax.random.split(key, 32)
    f32 = jnp.float32
    L = DEPTH

    def nrm(k, shape, scale):
        return jax.random.normal(k, shape, f32) * scale

    def gain(k, shape):
        return 1.0 + 0.05 * jax.random.normal(k, shape, f32)

    dt_init = jnp.exp(jax.random.uniform(ks[9], (L, SSM_HEADS), f32, math.log(1e-3), math.log(1e-1)))
    return {
        'x': nrm(ks[0], (BATCH, SEQ, D_MODEL), 1.0),
        'attn_norm_g': gain(ks[1], (L, D_MODEL)),
        'w_in': nrm(ks[2], (L, D_MODEL, IN_DIM), D_MODEL ** -0.5),
        'gm_v_norm_g': gain(ks[3], (L, GM_WIDTH)),
        'gm_w_s': nrm(ks[4], (L, GM_GROUPS, GM_CHUNK, GM_CHUNK), GM_CHUNK ** -0.5),
        'gm_b_s': 1.0 + 0.1 * jax.random.normal(ks[5], (L, GM_GROUPS, GM_CHUNK), f32),
        'ssm_conv_w': nrm(ks[6], (L, SSM_CONV_DIM, SSM_CONV), SSM_CONV ** -0.5),
        'ssm_conv_b': nrm(ks[7], (L, SSM_CONV_DIM), 0.02),
        'ssm_dt_bias': dt_init + jnp.log(-jnp.expm1(-dt_init)),
        'ssm_a_log': jnp.log(jax.random.uniform(ks[10], (L, SSM_HEADS), f32, 1.0, 16.0)),
        'ssm_d': gain(ks[11], (L, SSM_HEADS)),
        'ssm_norm_g': gain(ks[12], (L, SSM_INNER)),
        'da_q_norm_g': gain(ks[13], (L, DA_HEAD_DIM)),
        'da_k_norm_g': gain(ks[14], (L, DA_HEAD_DIM)),
        'da_lambda': nrm(ks[15], (L, 4, DA_HEAD_DIM), 0.1),
        'da_subln_g': gain(ks[16], (L, DA_V_DIM)),
        'w_branch_a': nrm(ks[17], (L, GM_WIDTH, D_MODEL), GM_WIDTH ** -0.5),
        'w_branch_b': nrm(ks[18], (L, SSM_INNER, D_MODEL), SSM_INNER ** -0.5),
        'w_branch_c': nrm(ks[19], (L, DA_WIDTH, D_MODEL), DA_WIDTH ** -0.5),
        'w_out': nrm(ks[20], (L, D_MODEL, D_MODEL), D_MODEL ** -0.5),
        'ffn_norm_g': gain(ks[21], (L, D_MODEL)),
        'ffn_w_up': nrm(ks[22], (L, D_MODEL, 2 * FFN_DIM), D_MODEL ** -0.5),
        'ffn_conv_w': nrm(ks[23], (L, 2 * FFN_DIM, FFN_CONV), FFN_CONV ** -0.5),
        'ffn_conv_b': nrm(ks[24], (L, 2 * FFN_DIM), 0.02),
        'ffn_w_down': nrm(ks[25], (L, FFN_DIM, D_MODEL), FFN_DIM ** -0.5),
    }


def reference(x, attn_norm_g, w_in, gm_v_norm_g, gm_w_s, gm_b_s, ssm_conv_w, ssm_conv_b,
              ssm_dt_bias, ssm_a_log, ssm_d, ssm_norm_g, da_q_norm_g, da_k_norm_g, da_lambda,
              da_subln_g, w_branch_a, w_branch_b, w_branch_c, w_out, ffn_norm_g, ffn_w_up,
              ffn_conv_w, ffn_conv_b, ffn_w_down):
    bsz, s, _ = x.shape
    cos, sin = rope_tables(s)
    for layer in range(DEPTH):
        lam_init = 0.8 - 0.6 * math.exp(-0.3 * layer)
        h = rmsnorm(x, attn_norm_g[layer])
        proj = h @ w_in[layer]
        gm_u, gm_v, ssm_z, ssm_xbc, ssm_dt, da_q, da_k, da_v, gates = split_cols(proj, IN_SIZES)
        y_a = chunked_spatial_gating(jax.nn.gelu(gm_u), jax.nn.gelu(gm_v), gm_v_norm_g[layer],
                                     gm_w_s[layer], gm_b_s[layer])
        y_b = mamba2_mixer(ssm_z, ssm_xbc, ssm_dt, ssm_conv_w[layer], ssm_conv_b[layer],
                           ssm_dt_bias[layer], ssm_a_log[layer], ssm_d[layer], ssm_norm_g[layer])
        y_c = diff_attention(da_q, da_k, da_v, da_q_norm_g[layer], da_k_norm_g[layer],
                             da_lambda[layer], da_subln_g[layer], lam_init, cos, sin)
        g = jax.nn.sigmoid(gates).reshape(bsz, s, N_BRANCH, D_MODEL)
        merged = (g[:, :, 0] * (y_a @ w_branch_a[layer])
                  + g[:, :, 1] * (y_b @ w_branch_b[layer])
                  + g[:, :, 2] * (y_c @ w_branch_c[layer]))
        x = x + merged @ w_out[layer]
        h = rmsnorm(x, ffn_norm_g[layer])
        up = causal_dwconv(h @ ffn_w_up[layer], ffn_conv_w[layer], ffn_conv_b[layer])
        gate, val = jnp.split(up, 2, axis=-1)
        x = x + (jax.nn.silu(gate) * val) @ ffn_w_down[layer]
    return x
```

```python
import functools
import math

import jax
import jax.numpy as jnp
import numpy as np
from jax import lax
from jax.experimental import pallas as pl
from jax.experimental.pallas import tpu as pltpu

F32 = jnp.float32
BF16 = jnp.bfloat16

D_MODEL = 1024
EPS = 1e-6
GM_WIDTH = 512
GM_GROUPS = 4
GM_CHUNK = 128
SSM_INNER = 1024
SSM_HEAD_DIM = 64
SSM_HEADS = SSM_INNER // SSM_HEAD_DIM
SSM_GROUPS = 2
SSM_STATE = 128
SSM_CONV = 4
SSM_CONV_DIM = SSM_INNER + 2 * SSM_GROUPS * SSM_STATE
SSM_CHUNK = 128
DA_HEADS = 4
DA_HEAD_DIM = 64
DA_V_DIM = 2 * DA_HEAD_DIM
DA_QK_WIDTH = DA_HEADS * 2 * DA_HEAD_DIM
DA_WIDTH = DA_HEADS * DA_V_DIM
ROPE_DIM = DA_HEAD_DIM // 4
ROPE_THETA = 500000.0
FFN_DIM = 2816
FFN_CONV = 3
DT_OFF = 2 * GM_WIDTH + SSM_INNER + SSM_CONV_DIM

LANE = 128
SUBLANE = 8
VMEM_LIMIT_BYTES = 56 * 1024 * 1024

P_WIDTH = 8192
P_U, P_V, P_Z, P_XS, P_BC, P_Q, P_K, P_VV, P_G = 0, 512, 1024, 2048, 3072, 3584, 4096, 4608, 5120
TN = 1024


def _dot(a, b):
    return jnp.dot(a, b, preferred_element_type=F32)


def _dot_nt(a, b):
    return lax.dot_general(a, b, (((1,), (1,)), ((), ())), preferred_element_type=F32)


def _sigmoid(x):
    return 1.0 / (1.0 + jnp.exp(-x))


def _silu(x):
    return x * _sigmoid(x)


def _gelu_tanh(x):
    c = math.sqrt(2.0 / math.pi)
    return 0.5 * x * (1.0 + jnp.tanh(c * (x + 0.044715 * (x * x * x))))


def _softplus(x):
    return jnp.maximum(x, 0.0) + jnp.log(1.0 + jnp.exp(-jnp.abs(x)))


def _split_bf16(x, parts):
    out = []
    r = x
    for _ in range(parts):
        p = r.astype(BF16)
        out.append(p)
        r = r - p.astype(F32)
    return out


def _dot_split(x, m, parts):
    return sum(_dot(p, m) for p in _split_bf16(x, parts))


def _dot_split_left(m, x, parts):
    return sum(_dot(m, p) for p in _split_bf16(x, parts))


def _in_proj_kernel(x_ref, g_ref, w_ref, wdt_ref, dtb_ref, vng_ref, cw_ref, cb_ref, qg_ref, kg_ref,
                    rc_ref, rs1_ref, rs2_ref, bd_ref,
                    p_ref, dt_ref, vt_ref,
                    h_scr, cbuf_scr, carry_scr, *, tm, tk, tiles_per_seq):
    i = pl.program_id(0)
    j = pl.program_id(1)

    @pl.when(j == 0)
    def _():
        x = x_ref[...]
        h = x * lax.rsqrt(jnp.mean(x * x, axis=-1, keepdims=True) + EPS) * g_ref[...]
        hb = h.astype(BF16)
        h_scr[...] = hb
        dt_ref[...] = _softplus(_dot(hb, wdt_ref[...]) + dtb_ref[...])

    acc = _dot(h_scr[...], w_ref[...])

    def conv_silu(raw, col0, width):
        first = (i % tiles_per_seq) == 0
        prev = jnp.where(first, 0.0, carry_scr[:, col0:col0 + width])
        cbuf_scr[0:SUBLANE, 0:width] = prev
        cbuf_scr[SUBLANE:SUBLANE + tm, 0:width] = raw
        carry_scr[:, col0:col0 + width] = raw[tm - SUBLANE:tm, :]
        y = cb_ref[:, col0:col0 + width]
        for k in range(SSM_CONV):
            off = SUBLANE - (SSM_CONV - 1) + k
            y = y + cw_ref[k:k + 1, col0:col0 + width] * cbuf_scr[off:off + tm, 0:width]
        return _silu(y)

    def qk_norm_rope(t, gain_ref, scale):
        outs = []
        for c in range(t.shape[1] // LANE):
            tc = t[:, c * LANE:(c + 1) * LANE]
            ms = _dot((tc * tc).astype(BF16), bd_ref[...]) * (1.0 / DA_HEAD_DIM)
            tn = tc * lax.rsqrt(ms + EPS) * gain_ref[...]
            half = ROPE_DIM // 2
            rot = (tn * rc_ref[...]
                   + pltpu.roll(tn, LANE - half, 1) * rs1_ref[...]
                   + pltpu.roll(tn, half, 1) * rs2_ref[...])
            outs.append(rot * scale)
        return jnp.concatenate(outs, axis=1)

    half_w = TN // 2

    @pl.when(j == P_U // TN)
    def _():
        p_ref[:, :half_w] = _gelu_tanh(acc[:, :half_w]).astype(BF16)
        gv = _gelu_tanh(acc[:, half_w:])
        vn = gv * lax.rsqrt(jnp.mean(gv * gv, axis=-1, keepdims=True) + EPS) * vng_ref[...]
        p_ref[:, half_w:] = vn.astype(BF16)

    @pl.when(j == P_Z // TN)
    def _():
        p_ref[...] = _silu(acc).astype(BF16)

    @pl.when(j == P_XS // TN)
    def _():
        p_ref[...] = conv_silu(acc, 0, TN).astype(BF16)

    @pl.when(j == P_BC // TN)
    def _():
        p_ref[:, :half_w] = conv_silu(acc[:, :half_w], SSM_INNER, half_w).astype(BF16)
        p_ref[:, half_w:] = qk_norm_rope(acc[:, half_w:], qg_ref, DA_HEAD_DIM ** -0.5).astype(BF16)

    @pl.when(j == P_K // TN)
    def _():
        p_ref[:, :half_w] = qk_norm_rope(acc[:, :half_w], kg_ref, 1.0).astype(BF16)
        v = acc[:, half_w:]
        p_ref[:, half_w:] = v.astype(BF16)
        for c in range(tm // tk):
            vt_ref[c] = v[c * tk:(c + 1) * tk, :].T.astype(BF16)

    @pl.when(j >= P_G // TN)
    def _():
        p_ref[...] = _sigmoid(acc).astype(BF16)


def _in_proj(x2d, seq, norm_g, w_in, gm_vg, conv_w, conv_b, dt_bias, q_g, k_g, rope_tabs, bd, *, tm, tk):
    tokens = x2d.shape[0]
    tiles_per_seq = seq // tm
    w_main = jnp.concatenate([w_in[:, :DT_OFF], w_in[:, DT_OFF + SSM_HEADS:]], axis=1).astype(BF16)
    w_dt = jnp.pad(w_in[:, DT_OFF:DT_OFF + SSM_HEADS], ((0, 0), (0, LANE - SSM_HEADS))).astype(BF16)
    dtb = jnp.pad(dt_bias, (0, LANE - SSM_HEADS)).reshape(1, LANE)
    rc, rs1, rs2 = rope_tabs
    const = lambda i, j: (0, 0)
    rope_map = lambda i, j: (i % tiles_per_seq, 0)
    kern = functools.partial(_in_proj_kernel, tm=tm, tk=tk, tiles_per_seq=tiles_per_seq)
    return pl.pallas_call(
        kern,
        grid=(tokens // tm, P_WIDTH // TN),
        in_specs=[
            pl.BlockSpec((tm, D_MODEL), lambda i, j: (i, 0)),
            pl.BlockSpec((1, D_MODEL), const),
            pl.BlockSpec((D_MODEL, TN), lambda i, j: (0, j)),
            pl.BlockSpec((D_MODEL, LANE), const),
            pl.BlockSpec((1, LANE), const),
            pl.BlockSpec((1, GM_WIDTH), const),
            pl.BlockSpec((SSM_CONV, SSM_CONV_DIM), const),
            pl.BlockSpec((1, SSM_CONV_DIM), const),
            pl.BlockSpec((1, LANE), const),
            pl.BlockSpec((1, LANE), const),
            pl.BlockSpec((tm, LANE), rope_map),
            pl.BlockSpec((tm, LANE), rope_map),
            pl.BlockSpec((tm, LANE), rope_map),
            pl.BlockSpec((LANE, LANE), const),
        ],
        out_specs=[
            pl.BlockSpec((tm, TN), lambda i, j: (i, j)),
            pl.BlockSpec((tm, LANE), lambda i, j: (i, 0)),
            pl.BlockSpec((tm // tk, DA_WIDTH, tk), lambda i, j: (i, 0, 0)),
        ],
        out_shape=[
            jax.ShapeDtypeStruct((tokens, P_WIDTH), BF16),
            jax.ShapeDtypeStruct((tokens, LANE), F32),
            jax.ShapeDtypeStruct((tokens // tk, DA_WIDTH, tk), BF16),
        ],
        scratch_shapes=[
            pltpu.VMEM((tm, D_MODEL), BF16),
            pltpu.VMEM((tm + SUBLANE, TN), F32),
            pltpu.VMEM((SUBLANE, SSM_CONV_DIM), F32),
        ],
        compiler_params=pltpu.CompilerParams(
            dimension_semantics=("arbitrary", "arbitrary"), vmem_limit_bytes=VMEM_LIMIT_BYTES),
        name="in_proj",
    )(x2d, norm_g.reshape(1, D_MODEL), w_main, w_dt, dtb, gm_vg.reshape(1, GM_WIDTH),
      conv_w.T, conv_b.reshape(1, SSM_CONV_DIM),
      jnp.tile(q_g, LANE // DA_HEAD_DIM).reshape(1, LANE), jnp.tile(k_g, LANE // DA_HEAD_DIM).reshape(1, LANE),
      rc, rs1, rs2, bd)


def _rope_tables(seq):
    half = ROPE_DIM // 2
    pos = jnp.arange(seq, dtype=F32)
    inv_freq = 1.0 / (ROPE_THETA ** (jnp.arange(0, ROPE_DIM, 2, dtype=F32) / ROPE_DIM))
    ang = pos[:, None] * inv_freq[None, :]
    cos, sin = jnp.cos(ang), jnp.sin(ang)
    ones = jnp.ones((seq, DA_HEAD_DIM - ROPE_DIM), F32)
    zeros = jnp.zeros((seq, DA_HEAD_DIM - ROPE_DIM), F32)
    zh = jnp.zeros((seq, half), F32)
    rc = jnp.concatenate([cos, cos, ones], axis=1)
    rs1 = jnp.concatenate([-sin, zh, zeros], axis=1)
    rs2 = jnp.concatenate([zh, sin, zeros], axis=1)
    rep = LANE // DA_HEAD_DIM
    return tuple(jnp.tile(t, (1, rep)) for t in (rc, rs1, rs2))


def _ssd_kernel(xs_ref, bc_ref, z_ref, dt_ref, alog_ref, dexp_ref, ng_ref, tril_ref, triu1_ref, e_ref,
                y_ref, state_scr, *, n_chunks):
    q = SSM_CHUNK
    n = SSM_STATE
    gw = SSM_INNER // SSM_GROUPS
    hpg = SSM_HEADS // SSM_GROUPS

    @pl.when(pl.program_id(1) == 0)
    def _():
        state_scr[...] = jnp.zeros_like(state_scr)

    a_row = -jnp.exp(alog_ref[...])
    row = lax.broadcasted_iota(jnp.int32, (q, q), 0)
    col = lax.broadcasted_iota(jnp.int32, (q, q), 1)
    causal = row >= col
    left = col < SSM_HEAD_DIM

    for cc in range(n_chunks):
        rows = slice(cc * q, (cc + 1) * q)
        xs = xs_ref[rows, :]
        dt = dt_ref[rows, :]
        dta = dt * a_row
        acum = _dot_split_left(tril_ref[...], dta, 3)
        ct = _dot_split(dta.T, triu1_ref[...], 3)
        acum_t = ct[:, :q]
        tot_t = ct[:, q:]
        dt_t = dt.T
        w_t = jnp.exp(tot_t - acum_t) * dt_t
        e_exp = _dot_split(jnp.exp(acum), e_ref[...], 2)

        y_parts = []
        st_parts = []
        for g in range(SSM_GROUPS):
            bm = bc_ref[rows, g * n:(g + 1) * n]
            cm = bc_ref[rows, SSM_GROUPS * n + g * n:SSM_GROUPS * n + (g + 1) * n]
            cb = _dot_nt(cm, bm)
            bm_t = bm.astype(F32).T
            prev = state_scr[:, g * gw:(g + 1) * gw].astype(BF16)
            y_parts.append(_dot(cm, prev) * e_exp[:, g * gw:(g + 1) * gw])
            for pair in range(hpg // 2):
                lhs_top = []
                lhs_bot = []
                for hh in range(2):
                    h = g * hpg + 2 * pair + hh
                    seg = jnp.broadcast_to(acum[:, h:h + 1], (q, q)) - acum_t[h:h + 1, :]
                    decay = jnp.exp(jnp.where(causal, seg, -jnp.inf))
                    lhs_top.append((cb * decay * dt_t[h:h + 1, :]).astype(BF16))
                    lhs_bot.append((bm_t * w_t[h:h + 1, :]).astype(BF16))
                lhs = jnp.concatenate(
                    [jnp.concatenate(lhs_top, axis=1), jnp.concatenate(lhs_bot, axis=1)], axis=0)
                c0 = g * gw + pair * LANE
                x2 = xs[:, c0:c0 + LANE]
                zero = jnp.zeros_like(x2)
                rhs = jnp.concatenate([jnp.where(left, x2, zero), jnp.where(left, zero, x2)], axis=0)
                out = _dot(lhs, rhs)
                y_parts.append(out[:q])
                st_parts.append(out[q:])
        y_off = jnp.concatenate([y_parts[0], y_parts[1 + hpg // 2]], axis=1)
        y_diag = jnp.concatenate(y_parts[1:1 + hpg // 2] + y_parts[2 + hpg // 2:], axis=1)
        st = jnp.concatenate(st_parts, axis=1)
        state_scr[...] = state_scr[...] * e_exp[q - 1:q, :] + st
        y = y_diag + y_off + dexp_ref[...] * xs.astype(F32)
        gated = y * z_ref[rows, :].astype(F32)
        out = gated * lax.rsqrt(jnp.mean(gated * gated, axis=-1, keepdims=True) + EPS) * ng_ref[...]
        y_ref[rows, :] = out.astype(BF16)


def _ssd(p, dt, batch, seq, a_log, d_skip, norm_g, consts, *, n_chunks):
    tokens = p.shape[0]
    step = n_chunks * SSM_CHUNK
    steps = seq // step
    tril, triu1, expand = consts
    alog = jnp.pad(a_log, (0, LANE - SSM_HEADS)).reshape(1, LANE)
    dexp = jnp.repeat(d_skip, SSM_HEAD_DIM).reshape(1, SSM_INNER)
    const = lambda b, c: (0, 0)
    return pl.pallas_call(
        functools.partial(_ssd_kernel, n_chunks=n_chunks),
        grid=(batch, steps),
        in_specs=[
            pl.BlockSpec((step, SSM_INNER), lambda b, c: (b * steps + c, P_XS // SSM_INNER)),
            pl.BlockSpec((step, 2 * SSM_GROUPS * SSM_STATE), lambda b, c: (b * steps + c, P_BC // 512)),
            pl.BlockSpec((step, SSM_INNER), lambda b, c: (b * steps + c, P_Z // SSM_INNER)),
            pl.BlockSpec((step, LANE), lambda b, c: (b * steps + c, 0)),
            pl.BlockSpec((1, LANE), const),
            pl.BlockSpec((1, SSM_INNER), const),
            pl.BlockSpec((1, SSM_INNER), const),
            pl.BlockSpec((SSM_CHUNK, SSM_CHUNK), const),
            pl.BlockSpec((SSM_CHUNK, 2 * SSM_CHUNK), const),
            pl.BlockSpec((LANE, SSM_INNER), const),
        ],
        out_specs=pl.BlockSpec((step, SSM_INNER), lambda b, c: (b * steps + c, 0)),
        out_shape=jax.ShapeDtypeStruct((tokens, SSM_INNER), BF16),
        scratch_shapes=[pltpu.VMEM((SSM_STATE, SSM_INNER), F32)],
        compiler_params=pltpu.CompilerParams(
            dimension_semantics=("arbitrary", "arbitrary"), vmem_limit_bytes=VMEM_LIMIT_BYTES),
        name="ssd",
    )(p, p, p, dt, alog, dexp, norm_g.reshape(1, SSM_INNER), tril, triu1, expand)


def _ssd_consts():
    q = SSM_CHUNK
    r = np.arange(q)
    tril = (r[:, None] >= r[None, :]).astype(np.float32)
    triu1 = np.concatenate([tril.T, np.ones((q, q), np.float32)], axis=1)
    expand = np.zeros((LANE, SSM_INNER), np.float32)
    for h in range(SSM_HEADS):
        expand[h, h * SSM_HEAD_DIM:(h + 1) * SSM_HEAD_DIM] = 1.0
    return tuple(jnp.asarray(a, BF16) for a in (tril, triu1, expand))


def _diff_attn_kernel(q_ref, k_ref, vt_ref, lam_ref, g_ref, o_ref, acc_scr, *, tq, lam_init):
    qi = pl.program_id(2)
    tk = tq
    q = q_ref[...]
    lane = lax.broadcasted_iota(jnp.int32, q.shape, 1)
    zero = jnp.zeros_like(q)
    qm = (jnp.where(lane < DA_HEAD_DIM, q, zero), jnp.where(lane < DA_HEAD_DIM, zero, q))
    acc_scr[...] = jnp.zeros_like(acc_scr)

    def step(j, carry, masked):
        start = pl.multiple_of(j * tk, tk)
        kc = k_ref[pl.ds(start, tk), :]
        vt = vt_ref[j]
        new = []
        for m in range(2):
            m_old, l_old = carry[2 * m], carry[2 * m + 1]
            s = _dot_nt(kc, qm[m])
            if masked:
                kk = lax.broadcasted_iota(jnp.int32, s.shape, 0)
                qq = lax.broadcasted_iota(jnp.int32, s.shape, 1)
                s = jnp.where(kk <= qq, s, -jnp.inf)
            m_new = jnp.maximum(m_old, jnp.max(s, axis=0, keepdims=True))
            alpha = jnp.exp(m_old - m_new)
            p = jnp.exp(s - m_new)
            l_new = l_old * alpha + jnp.sum(p, axis=0, keepdims=True)
            acc_scr[m] = acc_scr[m] * alpha + _dot(vt, p.astype(BF16))
            new += [m_new, l_new]
        return tuple(new)

    neg = jnp.full((1, tq), -jnp.inf, F32)
    zl = jnp.zeros((1, tq), F32)
    carry = lax.fori_loop(0, qi, lambda j, c: step(j, c, False), (neg, zl, neg, zl))
    _, l0, _, l1 = step(qi, carry, True)

    lv = lam_ref[...]
    lam = (jnp.exp(jnp.sum(lv[0:1] * lv[1:2], axis=-1, keepdims=True))
           - jnp.exp(jnp.sum(lv[2:3] * lv[3:4], axis=-1, keepdims=True)) + lam_init)
    o = acc_scr[0] * (1.0 / l0) - acc_scr[1] * (lam / l1)
    on = o * lax.rsqrt(jnp.mean(o * o, axis=0, keepdims=True) + EPS)
    o_ref[...] = (on.T * g_ref[...] * (1.0 - lam_init)).astype(BF16)


def _diff_attn(p, vt, batch, seq, lam_vecs, subln_g, lam_init, *, tq):
    tokens = p.shape[0]
    nq = seq // tq
    return pl.pallas_call(
        functools.partial(_diff_attn_kernel, tq=tq, lam_init=lam_init),
        grid=(batch, DA_HEADS, nq),
        in_specs=[
            pl.BlockSpec((tq, LANE), lambda b, h, i: (b * nq + i, P_Q // LANE + h)),
            pl.BlockSpec((seq, LANE), lambda b, h, i: (b, P_K // LANE + h)),
            pl.BlockSpec((nq, DA_V_DIM, tq), lambda b, h, i: (b, h, 0)),
            pl.BlockSpec((4, DA_HEAD_DIM), lambda b, h, i: (0, 0)),
            pl.BlockSpec((1, DA_V_DIM), lambda b, h, i: (0, 0)),
        ],
        out_specs=pl.BlockSpec((tq, DA_V_DIM), lambda b, h, i: (b * nq + i, h)),
        out_shape=jax.ShapeDtypeStruct((tokens, DA_WIDTH), BF16),
        scratch_shapes=[pltpu.VMEM((2, DA_V_DIM, tq), F32)],
        compiler_params=pltpu.CompilerParams(
            dimension_semantics=("arbitrary", "arbitrary", "arbitrary"), vmem_limit_bytes=VMEM_LIMIT_BYTES),
        name="diff_attn",
    )(p, p, vt, lam_vecs, subln_g.reshape(1, DA_V_DIM))


def _merge_kernel(u_ref, vn_ref, yb_ref, yc_ref, g0_ref, g1_ref, g2_ref, x_ref,
                  ws_ref, bexp_ref, wa_ref, wb_ref, wc_ref, wo_ref, o_ref, *, tm):
    nch = tm // GM_CHUNK
    gd = GM_WIDTH // GM_GROUPS
    vn = vn_ref[...]
    cols = []
    for g in range(GM_GROUPS):
        rhs = jnp.concatenate([vn[c * GM_CHUNK:(c + 1) * GM_CHUNK, g * gd:(g + 1) * gd] for c in range(nch)], axis=1)
        cols.append(_dot(ws_ref[g], rhs))
    mixed = jnp.concatenate(
        [jnp.concatenate([cols[g][:, c * gd:(c + 1) * gd] for g in range(GM_GROUPS)], axis=1) + bexp_ref[...]
         for c in range(nch)], axis=0)
    ya = (u_ref[...].astype(F32) * mixed).astype(BF16)
    merged = (g0_ref[...].astype(F32) * _dot(ya, wa_ref[...])
              + g1_ref[...].astype(F32) * _dot(yb_ref[...], wb_ref[...])
              + g2_ref[...].astype(F32) * _dot(yc_ref[...], wc_ref[...]))
    o_ref[...] = x_ref[...] + _dot(merged.astype(BF16), wo_ref[...])


def _merge(p, yb, yc, x2d, w_s, b_s, wa, wb, wc, wo, *, tm):
    tokens = x2d.shape[0]
    r = np.arange(GM_CHUNK)
    causal = jnp.asarray(r[:, None] >= r[None, :])
    ws = jnp.where(causal[None], w_s, 0.0).astype(BF16)
    bexp = jnp.repeat(b_s.T, GM_WIDTH // GM_GROUPS, axis=1)
    row = lambda w: pl.BlockSpec((tm, w), lambda i: (i, 0))
    pblk = lambda w, off: pl.BlockSpec((tm, w), lambda i: (i, off // w))
    full = lambda a: pl.BlockSpec(a.shape, lambda i: (0,) * a.ndim)
    args = (ws, bexp, wa.astype(BF16), wb.astype(BF16), wc.astype(BF16), wo.astype(BF16))
    return pl.pallas_call(
        functools.partial(_merge_kernel, tm=tm),
        grid=(tokens // tm,),
        in_specs=[pblk(GM_WIDTH, P_U), pblk(GM_WIDTH, P_V), row(SSM_INNER), row(DA_WIDTH),
                  pblk(D_MODEL, P_G), pblk(D_MODEL, P_G + D_MODEL), pblk(D_MODEL, P_G + 2 * D_MODEL),
                  row(D_MODEL)] + [full(a) for a in args],
        out_specs=row(D_MODEL),
        out_shape=jax.ShapeDtypeStruct((tokens, D_MODEL), F32),
        compiler_params=pltpu.CompilerParams(
            dimension_semantics=("arbitrary",), vmem_limit_bytes=VMEM_LIMIT_BYTES),
        name="merge",
    )(p, p, yb, yc, p, p, p, x2d, *args)


def _ffn_kernel(x_ref, g_ref, wg_ref, wv_ref, cwg_ref, cwv_ref, cbg_ref, cbv_ref, wd_ref, o_ref,
                h_scr, cbuf_scr, carry_scr, acc_scr, *, tm, fc, tiles_per_seq):
    i = pl.program_id(0)
    c = pl.program_id(1)

    @pl.when(c == 0)
    def _():
        x = x_ref[...]
        h = x * lax.rsqrt(jnp.mean(x * x, axis=-1, keepdims=True) + EPS) * g_ref[...]
        h_scr[...] = h.astype(BF16)
        acc_scr[...] = x

    hb = h_scr[...]
    first = (i % tiles_per_seq) == 0

    def conv(raw, half, cw_ref, cb_ref):
        cols = slice(half * fc, (half + 1) * fc)
        prev = jnp.where(first, 0.0, carry_scr[c, :, cols])
        cbuf_scr[0:SUBLANE, cols] = prev
        cbuf_scr[SUBLANE:SUBLANE + tm, cols] = raw
        carry_scr[c, :, cols] = raw[tm - SUBLANE:tm, :]
        y = cb_ref[0]
        for k in range(FFN_CONV):
            off = SUBLANE - (FFN_CONV - 1) + k
            y = y + cw_ref[0, k:k + 1, :] * cbuf_scr[off:off + tm, cols]
        return y

    gate = conv(_dot(hb, wg_ref[...]), 0, cwg_ref, cbg_ref)
    val = conv(_dot(hb, wv_ref[...]), 1, cwv_ref, cbv_ref)
    act = (_silu(gate) * val).astype(BF16)
    acc_scr[...] += _dot(act, wd_ref[...])

    @pl.when(c == pl.num_programs(1) - 1)
    def _():
        o_ref[...] = acc_scr[...]


def _ffn(x2d, seq, norm_g, w_up, conv_w, conv_b, w_down, *, tm, fc):
    tokens = x2d.shape[0]
    nc = FFN_DIM // fc
    wu = w_up.astype(BF16)
    cw = conv_w.T.reshape(FFN_CONV, 2 * nc, fc).transpose(1, 0, 2)
    cb = conv_b.reshape(2 * nc, 1, fc)
    return pl.pallas_call(
        functools.partial(_ffn_kernel, tm=tm, fc=fc, tiles_per_seq=seq // tm),
        grid=(tokens // tm, nc),
        in_specs=[
            pl.BlockSpec((tm, D_MODEL), lambda i, c: (i, 0)),
            pl.BlockSpec((1, D_MODEL), lambda i, c: (0, 0)),
            pl.BlockSpec((D_MODEL, fc), lambda i, c: (0, c)),
            pl.BlockSpec((D_MODEL, fc), lambda i, c: (0, nc + c)),
            pl.BlockSpec((1, FFN_CONV, fc), lambda i, c: (c, 0, 0)),
            pl.BlockSpec((1, FFN_CONV, fc), lambda i, c: (nc + c, 0, 0)),
            pl.BlockSpec((1, 1, fc), lambda i, c: (c, 0, 0)),
            pl.BlockSpec((1, 1, fc), lambda i, c: (nc + c, 0, 0)),
            pl.BlockSpec((fc, D_MODEL), lambda i, c: (c, 0)),
        ],
        out_specs=pl.BlockSpec((tm, D_MODEL), lambda i, c: (i, 0)),
        out_shape=jax.ShapeDtypeStruct((tokens, D_MODEL), F32),
        scratch_shapes=[
            pltpu.VMEM((tm, D_MODEL), BF16),
            pltpu.VMEM((tm + SUBLANE, 2 * fc), F32),
            pltpu.VMEM((nc, SUBLANE, 2 * fc), F32),
            pltpu.VMEM((tm, D_MODEL), F32),
        ],
        compiler_params=pltpu.CompilerParams(
            dimension_semantics=("arbitrary", "arbitrary"), vmem_limit_bytes=VMEM_LIMIT_BYTES),
        name="ffn",
    )(x2d, norm_g.reshape(1, D_MODEL), wu, wu, cw, cw, cb, cb, w_down.astype(BF16))


def _tile(seq, want):
    t = min(want, seq)
    assert seq % t == 0, (seq, t)
    return t


def kernel(x, attn_norm_g, w_in, gm_v_norm_g, gm_w_s, gm_b_s, ssm_conv_w, ssm_conv_b, ssm_dt_bias, ssm_a_log, ssm_d, ssm_norm_g, da_q_norm_g, da_k_norm_g, da_lambda, da_subln_g, w_branch_a, w_branch_b, w_branch_c, w_out, ffn_norm_g, ffn_w_up, ffn_conv_w, ffn_conv_b, ffn_w_down):
    batch, seq, _ = x.shape
    depth = w_in.shape[0]
    tokens = batch * seq
    tq = _tile(seq, 512)
    tm_proj = _tile(seq, 1024)
    tm_merge = _tile(seq, 512)
    tm_ffn = _tile(seq, 1024)
    rope_tabs = _rope_tables(seq)
    r = np.arange(LANE)
    bd = jnp.asarray((r[:, None] // DA_HEAD_DIM) == (r[None, :] // DA_HEAD_DIM), BF16)
    ssd_consts = _ssd_consts()
    x2d = x.reshape(tokens, D_MODEL)
    for layer in range(depth):
        lam_init = 0.8 - 0.6 * math.exp(-0.3 * layer)
        p, dt, vt = _in_proj(x2d, seq, attn_norm_g[layer], w_in[layer], gm_v_norm_g[layer],
                             ssm_conv_w[layer], ssm_conv_b[layer], ssm_dt_bias[layer],
                             da_q_norm_g[layer], da_k_norm_g[layer], rope_tabs, bd, tm=tm_proj, tk=tq)
        yb = _ssd(p, dt, batch, seq, ssm_a_log[layer], ssm_d[layer], ssm_norm_g[layer], ssd_consts, n_chunks=2)
        yc = _diff_attn(p, vt, batch, seq, da_lambda[layer], da_subln_g[layer], lam_init, tq=tq)
        x2d = _merge(p, yb, yc, x2d, gm_w_s[layer], gm_b_s[layer], w_branch_a[layer], w_branch_b[layer],
                     w_branch_c[layer], w_out[layer], tm=tm_merge)
        x2d = _ffn(x2d, seq, ffn_norm_g[layer], ffn_w_up[layer], ffn_conv_w[layer], ffn_conv_b[layer],
                   ffn_w_down[layer], tm=tm_ffn, fc=256)
    return x2d.reshape(batch, seq, D_MODEL)
```

```python
import functools
import math

import jax
import jax.numpy as jnp
import numpy as np
from jax import lax
from jax.experimental import pallas as pl
from jax.experimental.pallas import tpu as pltpu

F32 = jnp.float32
BF16 = jnp.bfloat16

D_MODEL = 1024
EPS = 1e-6
GM_WIDTH = 512
GM_GROUPS = 4
GM_CHUNK = 128
SSM_INNER = 1024
SSM_HEAD_DIM = 64
SSM_HEADS = SSM_INNER // SSM_HEAD_DIM
SSM_GROUPS = 2
SSM_STATE = 128
SSM_CONV = 4
SSM_CONV_DIM = SSM_INNER + 2 * SSM_GROUPS * SSM_STATE
SSM_CHUNK = 128
DA_HEADS = 4
DA_HEAD_DIM = 64
DA_V_DIM = 2 * DA_HEAD_DIM
DA_QK_WIDTH = DA_HEADS * 2 * DA_HEAD_DIM
DA_WIDTH = DA_HEADS * DA_V_DIM
ROPE_DIM = DA_HEAD_DIM // 4
ROPE_THETA = 500000.0
FFN_DIM = 2816
FFN_CONV = 3
DT_OFF = 2 * GM_WIDTH + SSM_INNER + SSM_CONV_DIM
LOG2E = math.log2(math.e)

LANE = 128
SUBLANE = 8
VMEM_LIMIT_BYTES = 56 * 1024 * 1024

P_WIDTH = 8192
P_U, P_V, P_Z, P_XS, P_BC, P_Q, P_K, P_VV, P_G = 0, 512, 1024, 2048, 3072, 3584, 4096, 4608, 5120
TN = 1024
ROW_BLOCK = 256


def _dot(a, b):
    return jnp.dot(a, b, preferred_element_type=F32)


def _dot_nt(a, b):
    return lax.dot_general(a, b, (((1,), (1,)), ((), ())), preferred_element_type=F32)


def _sigmoid(x):
    return 1.0 / (1.0 + jnp.exp(-x))


def _silu(x):
    return x * _sigmoid(x)


def _gelu_tanh(x):
    c = math.sqrt(2.0 / math.pi)
    return 0.5 * x * (1.0 + jnp.tanh(c * (x + 0.044715 * (x * x * x))))


def _softplus(x):
    return jnp.maximum(x, 0.0) + jnp.log(1.0 + jnp.exp(-jnp.abs(x)))


def _split_bf16(x, parts):
    out = []
    r = x
    for _ in range(parts):
        p = r.astype(BF16)
        out.append(p)
        r = r - p.astype(F32)
    return out


def _dot_split(x, m, parts):
    return sum(_dot(p, m) for p in _split_bf16(x, parts))


def _dot_split_left(m, x, parts):
    return sum(_dot(m, p) for p in _split_bf16(x, parts))


def _in_proj_kernel(x_ref, g_ref, w_ref, wdt_ref, dtb_ref, vng_ref, cw_ref, cb_ref, qg_ref, kg_ref,
                    rc_ref, rs1_ref, rs2_ref, bd_ref,
                    p_ref, dt_ref, vt_ref,
                    h_scr, cbuf_scr, carry_scr, *, tm, tk, tiles_per_seq):
    i = pl.program_id(0)
    j = pl.program_id(1)

    @pl.when(j == 0)
    def _():
        x = x_ref[...]
        h = x * lax.rsqrt(jnp.mean(x * x, axis=-1, keepdims=True) + EPS) * g_ref[...]
        hb = h.astype(BF16)
        h_scr[...] = hb
        dt_ref[...] = _softplus(_dot(hb, wdt_ref[...]) + dtb_ref[...])

    first = (i % tiles_per_seq) == 0
    half_w = TN // 2

    def conv_silu(rb, raw, col0, width):
        r0 = SUBLANE + rb * ROW_BLOCK
        if rb == 0:
            cbuf_scr[0:SUBLANE, 0:width] = jnp.where(first, 0.0, carry_scr[:, col0:col0 + width])
        cbuf_scr[r0:r0 + ROW_BLOCK, 0:width] = raw
        if rb == tm // ROW_BLOCK - 1:
            carry_scr[:, col0:col0 + width] = raw[ROW_BLOCK - SUBLANE:, :]
        y = cb_ref[:, col0:col0 + width]
        for k in range(SSM_CONV):
            off = r0 - (SSM_CONV - 1) + k
            y = y + cw_ref[k:k + 1, col0:col0 + width] * cbuf_scr[off:off + ROW_BLOCK, 0:width]
        return _silu(y)

    def qk_norm_rope(rows, t, gain_ref, scale):
        outs = []
        half = ROPE_DIM // 2
        for c in range(t.shape[1] // LANE):
            tc = t[:, c * LANE:(c + 1) * LANE]
            ms = _dot((tc * tc).astype(BF16), bd_ref[...]) * (1.0 / DA_HEAD_DIM)
            tn = tc * lax.rsqrt(ms + EPS) * gain_ref[...]
            rot = (tn * rc_ref[rows, :]
                   + pltpu.roll(tn, LANE - half, 1) * rs1_ref[rows, :]
                   + pltpu.roll(tn, half, 1) * rs2_ref[rows, :])
            outs.append(rot * scale)
        return jnp.concatenate(outs, axis=1)

    def row_blocks(epilogue):
        nrb = tm // ROW_BLOCK
        rows = [slice(rb * ROW_BLOCK, (rb + 1) * ROW_BLOCK) for rb in range(nrb)]
        acc_next = _dot(h_scr[rows[0], :], w_ref[...])
        for rb in range(nrb):
            acc = acc_next
            if rb + 1 < nrb:
                acc_next = _dot(h_scr[rows[rb + 1], :], w_ref[...])
            epilogue(rb, rows[rb], acc)

    @pl.when(j == P_U // TN)
    def _():
        def epilogue(rb, rows, acc):
            p_ref[rows, :half_w] = _gelu_tanh(acc[:, :half_w]).astype(BF16)
            gv = _gelu_tanh(acc[:, half_w:])
            vn = gv * lax.rsqrt(jnp.mean(gv * gv, axis=-1, keepdims=True) + EPS) * vng_ref[...]
            p_ref[rows, half_w:] = vn.astype(BF16)
        row_blocks(epilogue)

    @pl.when(j == P_Z // TN)
    def _():
        def epilogue(rb, rows, acc):
            p_ref[rows, :] = _silu(acc).astype(BF16)
        row_blocks(epilogue)

    @pl.when(j == P_XS // TN)
    def _():
        def epilogue(rb, rows, acc):
            p_ref[rows, :] = conv_silu(rb, acc, 0, TN).astype(BF16)
        row_blocks(epilogue)

    @pl.when(j == P_BC // TN)
    def _():
        def epilogue(rb, rows, acc):
            p_ref[rows, :half_w] = conv_silu(rb, acc[:, :half_w], SSM_INNER, half_w).astype(BF16)
            p_ref[rows, half_w:] = qk_norm_rope(
                rows, acc[:, half_w:], qg_ref, DA_HEAD_DIM ** -0.5 * LOG2E).astype(BF16)
        row_blocks(epilogue)

    @pl.when(j == P_K // TN)
    def _():
        def epilogue(rb, rows, acc):
            p_ref[rows, :half_w] = qk_norm_rope(rows, acc[:, :half_w], kg_ref, 1.0).astype(BF16)
            v = acc[:, half_w:]
            p_ref[rows, half_w:] = v.astype(BF16)
            c, r = divmod(rb * ROW_BLOCK, tk)
            vt_ref[c, :, r:r + ROW_BLOCK] = v.T.astype(BF16)
        row_blocks(epilogue)

    @pl.when(j >= P_G // TN)
    def _():
        def epilogue(rb, rows, acc):
            p_ref[rows, :] = _sigmoid(acc).astype(BF16)
        row_blocks(epilogue)


def _in_proj(x2d, seq, norm_g, w_in, gm_vg, conv_w, conv_b, dt_bias, q_g, k_g, rope_tabs, bd, *, tm, tk):
    tokens = x2d.shape[0]
    tiles_per_seq = seq // tm
    w_main = jnp.concatenate([w_in[:, :DT_OFF], w_in[:, DT_OFF + SSM_HEADS:]], axis=1).astype(BF16)
    w_dt = jnp.pad(w_in[:, DT_OFF:DT_OFF + SSM_HEADS], ((0, 0), (0, LANE - SSM_HEADS))).astype(BF16)
    dtb = jnp.pad(dt_bias, (0, LANE - SSM_HEADS)).reshape(1, LANE)
    rc, rs1, rs2 = rope_tabs
    const = lambda i, j: (0, 0)
    rope_map = lambda i, j: (i % tiles_per_seq, 0)
    kern = functools.partial(_in_proj_kernel, tm=tm, tk=tk, tiles_per_seq=tiles_per_seq)
    return pl.pallas_call(
        kern,
        grid=(tokens // tm, P_WIDTH // TN),
        in_specs=[
            pl.BlockSpec((tm, D_MODEL), lambda i, j: (i, 0)),
            pl.BlockSpec((1, D_MODEL), const),
            pl.BlockSpec((D_MODEL, TN), lambda i, j: (0, j)),
            pl.BlockSpec((D_MODEL, LANE), const),
            pl.BlockSpec((1, LANE), const),
            pl.BlockSpec((1, GM_WIDTH), const),
            pl.BlockSpec((SSM_CONV, SSM_CONV_DIM), const),
            pl.BlockSpec((1, SSM_CONV_DIM), const),
            pl.BlockSpec((1, LANE), const),
            pl.BlockSpec((1, LANE), const),
            pl.BlockSpec((tm, LANE), rope_map),
            pl.BlockSpec((tm, LANE), rope_map),
            pl.BlockSpec((tm, LANE), rope_map),
            pl.BlockSpec((LANE, LANE), const),
        ],
        out_specs=[
            pl.BlockSpec((tm, TN), lambda i, j: (i, j)),
            pl.BlockSpec((tm, LANE), lambda i, j: (i, 0)),
            pl.BlockSpec((tm // tk, DA_WIDTH, tk), lambda i, j: (i, 0, 0)),
        ],
        out_shape=[
            jax.ShapeDtypeStruct((tokens, P_WIDTH), BF16),
            jax.ShapeDtypeStruct((tokens, LANE), F32),
            jax.ShapeDtypeStruct((tokens // tk, DA_WIDTH, tk), BF16),
        ],
        scratch_shapes=[
            pltpu.VMEM((tm, D_MODEL), BF16),
            pltpu.VMEM((tm + SUBLANE, TN), F32),
            pltpu.VMEM((SUBLANE, SSM_CONV_DIM), F32),
        ],
        compiler_params=pltpu.CompilerParams(
            dimension_semantics=("arbitrary", "arbitrary"), vmem_limit_bytes=VMEM_LIMIT_BYTES),
        name="in_proj",
    )(x2d, norm_g.reshape(1, D_MODEL), w_main, w_dt, dtb, gm_vg.reshape(1, GM_WIDTH),
      conv_w.T, conv_b.reshape(1, SSM_CONV_DIM),
      jnp.tile(q_g, LANE // DA_HEAD_DIM).reshape(1, LANE), jnp.tile(k_g, LANE // DA_HEAD_DIM).reshape(1, LANE),
      rc, rs1, rs2, bd)


def _rope_tables(seq):
    half = ROPE_DIM // 2
    pos = jnp.arange(seq, dtype=F32)
    inv_freq = 1.0 / (ROPE_THETA ** (jnp.arange(0, ROPE_DIM, 2, dtype=F32) / ROPE_DIM))
    ang = pos[:, None] * inv_freq[None, :]
    cos, sin = jnp.cos(ang), jnp.sin(ang)
    ones = jnp.ones((seq, DA_HEAD_DIM - ROPE_DIM), F32)
    zeros = jnp.zeros((seq, DA_HEAD_DIM - ROPE_DIM), F32)
    zh = jnp.zeros((seq, half), F32)
    rc = jnp.concatenate([cos, cos, ones], axis=1)
    rs1 = jnp.concatenate([-sin, zh, zeros], axis=1)
    rs2 = jnp.concatenate([zh, sin, zeros], axis=1)
    rep = LANE // DA_HEAD_DIM
    return tuple(jnp.tile(t, (1, rep)) for t in (rc, rs1, rs2))


def _ssd_kernel(xs_ref, bc_ref, z_ref, dt_ref, alog_ref, dexp_ref, ng_ref, tril_ref, triu1_ref, e_ref,
                y_ref, state_scr, *, n_chunks):
    q = SSM_CHUNK
    n = SSM_STATE
    gw = SSM_INNER // SSM_GROUPS
    hpg = SSM_HEADS // SSM_GROUPS

    @pl.when(pl.program_id(1) == 0)
    def _():
        state_scr[...] = jnp.zeros_like(state_scr)

    a_row = -jnp.exp(alog_ref[...])
    row = lax.broadcasted_iota(jnp.int32, (q, q), 0)
    col = lax.broadcasted_iota(jnp.int32, (q, q), 1)
    causal = row >= col
    left = col < SSM_HEAD_DIM

    for cc in range(n_chunks):
        rows = slice(cc * q, (cc + 1) * q)
        xs = xs_ref[rows, :]
        dt = dt_ref[rows, :]
        dta = dt * a_row
        acum = _dot_split_left(tril_ref[...], dta, 3)
        ct = _dot_split(dta.T, triu1_ref[...], 3)
        acum_t = ct[:, :q]
        tot_t = ct[:, q:]
        dt_t = dt.T
        w_t = jnp.exp(tot_t - acum_t) * dt_t
        e_exp = _dot_split(jnp.exp(acum), e_ref[...], 2)

        y_parts = []
        st_parts = []
        for g in range(SSM_GROUPS):
            bm = bc_ref[rows, g * n:(g + 1) * n]
            cm = bc_ref[rows, SSM_GROUPS * n + g * n:SSM_GROUPS * n + (g + 1) * n]
            cb = _dot_nt(cm, bm)
            bm_t = bm.astype(F32).T
            prev = state_scr[:, g * gw:(g + 1) * gw].astype(BF16)
            y_parts.append(_dot(cm, prev) * e_exp[:, g * gw:(g + 1) * gw])
            for pair in range(hpg // 2):
                lhs_top = []
                lhs_bot = []
                for hh in range(2):
                    h = g * hpg + 2 * pair + hh
                    seg = jnp.broadcast_to(acum[:, h:h + 1], (q, q)) - acum_t[h:h + 1, :]
                    decay = jnp.exp(jnp.where(causal, seg, -jnp.inf))
                    lhs_top.append((cb * decay * dt_t[h:h + 1, :]).astype(BF16))
                    lhs_bot.append((bm_t * w_t[h:h + 1, :]).astype(BF16))
                lhs = jnp.concatenate(
                    [jnp.concatenate(lhs_top, axis=1), jnp.concatenate(lhs_bot, axis=1)], axis=0)
                c0 = g * gw + pair * LANE
                x2 = xs[:, c0:c0 + LANE]
                zero = jnp.zeros_like(x2)
                rhs = jnp.concatenate([jnp.where(left, x2, zero), jnp.where(left, zero, x2)], axis=0)
                out = _dot(lhs, rhs)
                y_parts.append(out[:q])
                st_parts.append(out[q:])
        y_off = jnp.concatenate([y_parts[0], y_parts[1 + hpg // 2]], axis=1)
        y_diag = jnp.concatenate(y_parts[1:1 + hpg // 2] + y_parts[2 + hpg // 2:], axis=1)
        st = jnp.concatenate(st_parts, axis=1)
        state_scr[...] = state_scr[...] * e_exp[q - 1:q, :] + st
        y = y_diag + y_off + dexp_ref[...] * xs.astype(F32)
        gated = y * z_ref[rows, :].astype(F32)
        out = gated * lax.rsqrt(jnp.mean(gated * gated, axis=-1, keepdims=True) + EPS) * ng_ref[...]
        y_ref[rows, :] = out.astype(BF16)


def _ssd(p, dt, batch, seq, a_log, d_skip, norm_g, consts, *, n_chunks):
    tokens = p.shape[0]
    step = n_chunks * SSM_CHUNK
    steps = seq // step
    tril, triu1, expand = consts
    alog = jnp.pad(a_log, (0, LANE - SSM_HEADS)).reshape(1, LANE)
    dexp = jnp.repeat(d_skip, SSM_HEAD_DIM).reshape(1, SSM_INNER)
    const = lambda b, c: (0, 0)
    return pl.pallas_call(
        functools.partial(_ssd_kernel, n_chunks=n_chunks),
        grid=(batch, steps),
        in_specs=[
            pl.BlockSpec((step, SSM_INNER), lambda b, c: (b * steps + c, P_XS // SSM_INNER)),
            pl.BlockSpec((step, 2 * SSM_GROUPS * SSM_STATE), lambda b, c: (b * steps + c, P_BC // 512)),
            pl.BlockSpec((step, SSM_INNER), lambda b, c: (b * steps + c, P_Z // SSM_INNER)),
            pl.BlockSpec((step, LANE), lambda b, c: (b * steps + c, 0)),
            pl.BlockSpec((1, LANE), const),
            pl.BlockSpec((1, SSM_INNER), const),
            pl.BlockSpec((1, SSM_INNER), const),
            pl.BlockSpec((SSM_CHUNK, SSM_CHUNK), const),
            pl.BlockSpec((SSM_CHUNK, 2 * SSM_CHUNK), const),
            pl.BlockSpec((LANE, SSM_INNER), const),
        ],
        out_specs=pl.BlockSpec((step, SSM_INNER), lambda b, c: (b * steps + c, 0)),
        out_shape=jax.ShapeDtypeStruct((tokens, SSM_INNER), BF16),
        scratch_shapes=[pltpu.VMEM((SSM_STATE, SSM_INNER), F32)],
        compiler_params=pltpu.CompilerParams(
            dimension_semantics=("arbitrary", "arbitrary"), vmem_limit_bytes=VMEM_LIMIT_BYTES),
        name="ssd",
    )(p, p, p, dt, alog, dexp, norm_g.reshape(1, SSM_INNER), tril, triu1, expand)


def _ssd_consts():
    q = SSM_CHUNK
    r = np.arange(q)
    tril = (r[:, None] >= r[None, :]).astype(np.float32)
    triu1 = np.concatenate([tril.T, np.ones((q, q), np.float32)], axis=1)
    expand = np.zeros((LANE, SSM_INNER), np.float32)
    for h in range(SSM_HEADS):
        expand[h, h * SSM_HEAD_DIM:(h + 1) * SSM_HEAD_DIM] = 1.0
    return tuple(jnp.asarray(a, BF16) for a in (tril, triu1, expand))


MAX_STATIC_SHIFT = 40.0


def _diff_attn_kernel(bnd_ref, q_ref, k_ref, vt_ref, lam_ref, g_ref, o_ref, acc_scr, l_scr, *, tq, lam_init):
    qi = pl.program_id(2)
    tk = tq
    q = q_ref[...]
    lane = lax.broadcasted_iota(jnp.int32, q.shape, 1)
    zero = jnp.zeros_like(q)
    qm = (jnp.where(lane < DA_HEAD_DIM, q, zero), jnp.where(lane < DA_HEAD_DIM, zero, q))
    acc_scr[...] = jnp.zeros_like(acc_scr)

    def scores(j, m, masked):
        kc = k_ref[pl.ds(pl.multiple_of(j * tk, tk), tk), :]
        s = _dot_nt(kc, qm[m])
        if masked:
            kk = lax.broadcasted_iota(jnp.int32, s.shape, 0)
            qq = lax.broadcasted_iota(jnp.int32, s.shape, 1)
            s = jnp.where(kk <= qq, s, -jnp.inf)
        return s

    def finalize(l0, l1):
        lv = lam_ref[...]
        lam = (jnp.exp(jnp.sum(lv[0:1] * lv[1:2], axis=-1, keepdims=True))
               - jnp.exp(jnp.sum(lv[2:3] * lv[3:4], axis=-1, keepdims=True)) + lam_init)
        o = acc_scr[0] * (1.0 / l0) - acc_scr[1] * (lam / l1)
        on = o * lax.rsqrt(jnp.mean(o * o, axis=0, keepdims=True) + EPS)
        o_ref[...] = (on.T * g_ref[...] * (1.0 - lam_init)).astype(BF16)

    @pl.when(bnd_ref[1] > 0.5)
    def _():
        shift = bnd_ref[0]
        l_scr[...] = jnp.zeros_like(l_scr)

        def chunk(j, masked):
            vt = vt_ref[j]
            for m in range(2):
                p = jnp.exp2(scores(j, m, masked) - shift)
                l_scr[m] += jnp.sum(p.reshape(tk // SUBLANE, SUBLANE, tq), axis=0)
                acc_scr[m] += _dot(vt, p.astype(BF16))

        def body(j, carry):
            chunk(j, False)
            return carry

        lax.fori_loop(0, qi, body, 0)
        chunk(qi, True)
        finalize(jnp.sum(l_scr[0], axis=0, keepdims=True), jnp.sum(l_scr[1], axis=0, keepdims=True))

    @pl.when(bnd_ref[1] <= 0.5)
    def _():
        def chunk(j, stats, masked):
            vt = vt_ref[j]
            new = []
            for m in range(2):
                m_old, l_old = stats[2 * m], stats[2 * m + 1]
                s = scores(j, m, masked)
                m_new = jnp.maximum(m_old, jnp.max(s, axis=0, keepdims=True))
                alpha = jnp.exp2(m_old - m_new)
                p = jnp.exp2(s - m_new)
                acc_scr[m] = acc_scr[m] * alpha + _dot(vt, p.astype(BF16))
                new += [m_new, l_old * alpha + jnp.sum(p, axis=0, keepdims=True)]
            return tuple(new)

        neg = jnp.full((1, tq), -jnp.inf, F32)
        zl = jnp.zeros((1, tq), F32)
        stats = lax.fori_loop(0, qi, lambda j, st: chunk(j, st, False), (neg, zl, neg, zl))
        _, l0, _, l1 = chunk(qi, stats, True)
        finalize(l0, l1)


def _diff_attn(p, vt, batch, seq, q_g, k_g, lam_vecs, subln_g, lam_init, *, tq):
    tokens = p.shape[0]
    nq = seq // tq
    bound = 1.02 * LOG2E * math.sqrt(DA_HEAD_DIM) * jnp.max(jnp.abs(q_g)) * jnp.max(jnp.abs(k_g))
    bnd = jnp.stack([bound, (bound <= MAX_STATIC_SHIFT).astype(F32)]).astype(F32)
    return pl.pallas_call(
        functools.partial(_diff_attn_kernel, tq=tq, lam_init=lam_init),
        grid=(batch, DA_HEADS, nq),
        in_specs=[
            pl.BlockSpec(memory_space=pltpu.SMEM),
            pl.BlockSpec((tq, LANE), lambda b, h, i: (b * nq + i, P_Q // LANE + h)),
            pl.BlockSpec((seq, LANE), lambda b, h, i: (b, P_K // LANE + h)),
            pl.BlockSpec((nq, DA_V_DIM, tq), lambda b, h, i: (b, h, 0)),
            pl.BlockSpec((4, DA_HEAD_DIM), lambda b, h, i: (0, 0)),
            pl.BlockSpec((1, DA_V_DIM), lambda b, h, i: (0, 0)),
        ],
        out_specs=pl.BlockSpec((tq, DA_V_DIM), lambda b, h, i: (b * nq + i, h)),
        out_shape=jax.ShapeDtypeStruct((tokens, DA_WIDTH), BF16),
        scratch_shapes=[
            pltpu.VMEM((2, DA_V_DIM, tq), F32),
            pltpu.VMEM((2, SUBLANE, tq), F32),
        ],
        compiler_params=pltpu.CompilerParams(
            dimension_semantics=("arbitrary", "arbitrary", "arbitrary"), vmem_limit_bytes=VMEM_LIMIT_BYTES),
        name="diff_attn",
    )(bnd, p, p, vt, lam_vecs, subln_g.reshape(1, DA_V_DIM))


def _merge_kernel(u_ref, vn_ref, yb_ref, yc_ref, g0_ref, g1_ref, g2_ref, x_ref,
                  ws_ref, bexp_ref, wa_ref, wb_ref, wc_ref, wo_ref, o_ref, *, tm):
    nch = tm // GM_CHUNK
    gd = GM_WIDTH // GM_GROUPS
    vn = vn_ref[...]
    cols = []
    for g in range(GM_GROUPS):
        rhs = jnp.concatenate([vn[c * GM_CHUNK:(c + 1) * GM_CHUNK, g * gd:(g + 1) * gd] for c in range(nch)], axis=1)
        cols.append(_dot(ws_ref[g], rhs))
    mixed = jnp.concatenate(
        [jnp.concatenate([cols[g][:, c * gd:(c + 1) * gd] for g in range(GM_GROUPS)], axis=1) + bexp_ref[...]
         for c in range(nch)], axis=0)
    ya = (u_ref[...].astype(F32) * mixed).astype(BF16)
    merged = (g0_ref[...].astype(F32) * _dot(ya, wa_ref[...])
              + g1_ref[...].astype(F32) * _dot(yb_ref[...], wb_ref[...])
              + g2_ref[...].astype(F32) * _dot(yc_ref[...], wc_ref[...]))
    o_ref[...] = x_ref[...] + _dot(merged.astype(BF16), wo_ref[...])


def _merge(p, yb, yc, x2d, w_s, b_s, wa, wb, wc, wo, *, tm):
    tokens = x2d.shape[0]
    r = np.arange(GM_CHUNK)
    causal = jnp.asarray(r[:, None] >= r[None, :])
    ws = jnp.where(causal[None], w_s, 0.0).astype(BF16)
    bexp = jnp.repeat(b_s.T, GM_WIDTH // GM_GROUPS, axis=1)
    row = lambda w: pl.BlockSpec((tm, w), lambda i: (i, 0))
    pblk = lambda w, off: pl.BlockSpec((tm, w), lambda i: (i, off // w))
    full = lambda a: pl.BlockSpec(a.shape, lambda i: (0,) * a.ndim)
    args = (ws, bexp, wa.astype(BF16), wb.astype(BF16), wc.astype(BF16), wo.astype(BF16))
    return pl.pallas_call(
        functools.partial(_merge_kernel, tm=tm),
        grid=(tokens // tm,),
        in_specs=[pblk(GM_WIDTH, P_U), pblk(GM_WIDTH, P_V), row(SSM_INNER), row(DA_WIDTH),
                  pblk(D_MODEL, P_G), pblk(D_MODEL, P_G + D_MODEL), pblk(D_MODEL, P_G + 2 * D_MODEL),
                  row(D_MODEL)] + [full(a) for a in args],
        out_specs=row(D_MODEL),
        out_shape=jax.ShapeDtypeStruct((tokens, D_MODEL), F32),
        compiler_params=pltpu.CompilerParams(
            dimension_semantics=("arbitrary",), vmem_limit_bytes=VMEM_LIMIT_BYTES),
        name="merge",
    )(p, p, yb, yc, p, p, p, x2d, *args)


def _ffn_kernel(x_ref, g_ref, wg_ref, wv_ref, cwg_ref, cwv_ref, cbg_ref, cbv_ref, wd_ref, o_ref,
                h_scr, cbuf_scr, carry_scr, acc_scr, *, tm, fc, tiles_per_seq):
    i = pl.program_id(0)
    c = pl.program_id(1)

    @pl.when(c == 0)
    def _():
        x = x_ref[...]
        h = x * lax.rsqrt(jnp.mean(x * x, axis=-1, keepdims=True) + EPS) * g_ref[...]
        h_scr[...] = h.astype(BF16)
        acc_scr[...] = x

    first = (i % tiles_per_seq) == 0
    nrb = tm // ROW_BLOCK

    def conv(rb, raw, half, cw_ref, cb_ref):
        cols = slice(half * fc, (half + 1) * fc)
        r0 = SUBLANE + rb * ROW_BLOCK
        if rb == 0:
            cbuf_scr[0:SUBLANE, cols] = jnp.where(first, 0.0, carry_scr[c, :, cols])
        cbuf_scr[r0:r0 + ROW_BLOCK, cols] = raw
        if rb == nrb - 1:
            carry_scr[c, :, cols] = raw[ROW_BLOCK - SUBLANE:, :]
        y = cb_ref[0]
        for k in range(FFN_CONV):
            off = r0 - (FFN_CONV - 1) + k
            y = y + cw_ref[0, k:k + 1, :] * cbuf_scr[off:off + ROW_BLOCK, cols]
        return y

    def up(rb):
        hb = h_scr[rb * ROW_BLOCK:(rb + 1) * ROW_BLOCK, :]
        return _dot(hb, wg_ref[...]), _dot(hb, wv_ref[...])

    raw_next = up(0)
    for rb in range(nrb):
        raw_g, raw_v = raw_next
        if rb + 1 < nrb:
            raw_next = up(rb + 1)
        gate = conv(rb, raw_g, 0, cwg_ref, cbg_ref)
        val = conv(rb, raw_v, 1, cwv_ref, cbv_ref)
        act = (_silu(gate) * val).astype(BF16)
        acc_scr[rb * ROW_BLOCK:(rb + 1) * ROW_BLOCK, :] += _dot(act, wd_ref[...])

    @pl.when(c == pl.num_programs(1) - 1)
    def _():
        o_ref[...] = acc_scr[...]


def _ffn(x2d, seq, norm_g, w_up, conv_w, conv_b, w_down, *, tm, fc):
    tokens = x2d.shape[0]
    nc = FFN_DIM // fc
    wu = w_up.astype(BF16)
    cw = conv_w.T.reshape(FFN_CONV, 2 * nc, fc).transpose(1, 0, 2)
    cb = conv_b.reshape(2 * nc, 1, fc)
    return pl.pallas_call(
        functools.partial(_ffn_kernel, tm=tm, fc=fc, tiles_per_seq=seq // tm),
        grid=(tokens // tm, nc),
        in_specs=[
            pl.BlockSpec((tm, D_MODEL), lambda i, c: (i, 0)),
            pl.BlockSpec((1, D_MODEL), lambda i, c: (0, 0)),
            pl.BlockSpec((D_MODEL, fc), lambda i, c: (0, c)),
            pl.BlockSpec((D_MODEL, fc), lambda i, c: (0, nc + c)),
            pl.BlockSpec((1, FFN_CONV, fc), lambda i, c: (c, 0, 0)),
            pl.BlockSpec((1, FFN_CONV, fc), lambda i, c: (nc + c, 0, 0)),
            pl.BlockSpec((1, 1, fc), lambda i, c: (c, 0, 0)),
            pl.BlockSpec((1, 1, fc), lambda i, c: (nc + c, 0, 0)),
            pl.BlockSpec((fc, D_MODEL), lambda i, c: (c, 0)),
        ],
        out_specs=pl.BlockSpec((tm, D_MODEL), lambda i, c: (i, 0)),
        out_shape=jax.ShapeDtypeStruct((tokens, D_MODEL), F32),
        scratch_shapes=[
            pltpu.VMEM((tm, D_MODEL), BF16),
            pltpu.VMEM((tm + SUBLANE, 2 * fc), F32),
            pltpu.VMEM((nc, SUBLANE, 2 * fc), F32),
            pltpu.VMEM((tm, D_MODEL), F32),
        ],
        compiler_params=pltpu.CompilerParams(
            dimension_semantics=("arbitrary", "arbitrary"), vmem_limit_bytes=VMEM_LIMIT_BYTES),
        name="ffn",
    )(x2d, norm_g.reshape(1, D_MODEL), wu, wu, cw, cw, cb, cb, w_down.astype(BF16))


def _tile(seq, want):
    t = min(want, seq)
    assert seq % t == 0, (seq, t)
    return t


def kernel(x, attn_norm_g, w_in, gm_v_norm_g, gm_w_s, gm_b_s, ssm_conv_w, ssm_conv_b, ssm_dt_bias, ssm_a_log, ssm_d, ssm_norm_g, da_q_norm_g, da_k_norm_g, da_lambda, da_subln_g, w_branch_a, w_branch_b, w_branch_c, w_out, ffn_norm_g, ffn_w_up, ffn_conv_w, ffn_conv_b, ffn_w_down):
    batch, seq, _ = x.shape
    depth = w_in.shape[0]
    tokens = batch * seq
    tq = _tile(seq, 1024)
    tm_proj = _tile(seq, 1024)
    tm_merge = _tile(seq, 512)
    tm_ffn = _tile(seq, 1024)
    rope_tabs = _rope_tables(seq)
    r = np.arange(LANE)
    bd = jnp.asarray((r[:, None] // DA_HEAD_DIM) == (r[None, :] // DA_HEAD_DIM), BF16)
    ssd_consts = _ssd_consts()
    x2d = x.reshape(tokens, D_MODEL)
    for layer in range(depth):
        lam_init = 0.8 - 0.6 * math.exp(-0.3 * layer)
        p, dt, vt = _in_proj(x2d, seq, attn_norm_g[layer], w_in[layer], gm_v_norm_g[layer],
                             ssm_conv_w[layer], ssm_conv_b[layer], ssm_dt_bias[layer],
                             da_q_norm_g[layer], da_k_norm_g[layer], rope_tabs, bd, tm=tm_proj, tk=tq)
        yb = _ssd(p, dt, batch, seq, ssm_a_log[layer], ssm_d[layer], ssm_norm_g[layer], ssd_consts, n_chunks=2)
        yc = _diff_attn(p, vt, batch, seq, da_q_norm_g[layer], da_k_norm_g[layer], da_lambda[layer],
                        da_subln_g[layer], lam_init, tq=tq)
        x2d = _merge(p, yb, yc, x2d, gm_w_s[layer], gm_b_s[layer], w_branch_a[layer], w_branch_b[layer],
                     w_branch_c[layer], w_out[layer], tm=tm_merge)
        x2d = _ffn(x2d, seq, ffn_norm_g[layer], ffn_w_up[layer], ffn_conv_w[layer], ffn_conv_b[layer],
                   ffn_w_down[layer], tm=tm_ffn, fc=256)
    return x2d.reshape(batch, seq, D_MODEL)
```

```python
import functools
import math

import jax
import jax.numpy as jnp
import numpy as np
from jax import lax
from jax.experimental import pallas as pl
from jax.experimental.pallas import tpu as pltpu

F32 = jnp.float32
BF16 = jnp.bfloat16

D_MODEL = 1024
EPS = 1e-6
GM_WIDTH = 512
GM_GROUPS = 4
GM_CHUNK = 128
SSM_INNER = 1024
SSM_HEAD_DIM = 64
SSM_HEADS = SSM_INNER // SSM_HEAD_DIM
SSM_GROUPS = 2
SSM_STATE = 128
SSM_CONV = 4
SSM_CONV_DIM = SSM_INNER + 2 * SSM_GROUPS * SSM_STATE
SSM_CHUNK = 128
DA_HEADS = 4
DA_HEAD_DIM = 64
DA_V_DIM = 2 * DA_HEAD_DIM
DA_QK_WIDTH = DA_HEADS * 2 * DA_HEAD_DIM
DA_WIDTH = DA_HEADS * DA_V_DIM
ROPE_DIM = DA_HEAD_DIM // 4
ROPE_THETA = 500000.0
FFN_DIM = 2816
FFN_CONV = 3
DT_OFF = 2 * GM_WIDTH + SSM_INNER + SSM_CONV_DIM
LOG2E = math.log2(math.e)

LANE = 128
SUBLANE = 8
MXU_WIDTH = 256
VMEM_LIMIT_BYTES = 56 * 1024 * 1024

P_WIDTH = 8192
P_U, P_V, P_Z, P_XS, P_BC, P_Q, P_K, P_VV, P_G = 0, 512, 1024, 2048, 3072, 3584, 4096, 4608, 5120
TN = 1024
ROW_BLOCK = 256


def _dot(a, b):
    return jnp.dot(a, b, preferred_element_type=F32)


def _dot_cols(a, w_ref, col0, width):
    parts = [_dot(a, w_ref[:, c:c + MXU_WIDTH]) for c in range(col0, col0 + width, MXU_WIDTH)]
    return parts[0] if len(parts) == 1 else jnp.concatenate(parts, axis=1)


def _dot_nt(a, b):
    return lax.dot_general(a, b, (((1,), (1,)), ((), ())), preferred_element_type=F32)


def _sigmoid(x):
    return 1.0 / (1.0 + jnp.exp(-x))


def _silu(x):
    return x * _sigmoid(x)


def _gelu_tanh(x):
    c = math.sqrt(2.0 / math.pi)
    return 0.5 * x * (1.0 + jnp.tanh(c * (x + 0.044715 * (x * x * x))))


def _softplus(x):
    return jnp.maximum(x, 0.0) + jnp.log(1.0 + jnp.exp(-jnp.abs(x)))


def _split_bf16(x, parts):
    out = []
    r = x
    for _ in range(parts):
        p = r.astype(BF16)
        out.append(p)
        r = r - p.astype(F32)
    return out


def _dot_split(x, m, parts):
    return sum(_dot(p, m) for p in _split_bf16(x, parts))


def _dot_split_left(m, x, parts):
    return sum(_dot(m, p) for p in _split_bf16(x, parts))


def _pipelined(units, produce, consume, depth):
    queue = [produce(*units[k]) for k in range(min(depth, len(units)))]
    for u, unit in enumerate(units):
        value = queue.pop(0)
        if u + depth < len(units):
            queue.append(produce(*units[u + depth]))
        consume(unit, value)


def _in_proj_kernel(x_ref, g_ref, w_ref, wdt_ref, dtb_ref, vng_ref, cw_ref, cb_ref, qg_ref, kg_ref,
                    rc_ref, rs1_ref, rs2_ref, bd_ref,
                    p_ref, dt_ref, vt_ref,
                    h_scr, cbuf_scr, carry_scr, *, tm, tiles_per_seq):
    i = pl.program_id(0)
    first = (i % tiles_per_seq) == 0
    half_w = TN // 2
    nrb = tm // ROW_BLOCK
    rows = [slice(rb * ROW_BLOCK, (rb + 1) * ROW_BLOCK) for rb in range(nrb)]

    def norm_rows(rb):
        x = x_ref[rows[rb], :]
        h = x * lax.rsqrt(jnp.mean(x * x, axis=-1, keepdims=True) + EPS) * g_ref[...]
        hb = h.astype(BF16)
        h_scr[rows[rb], :] = hb
        dt_ref[rows[rb], :] = _softplus(_dot(hb, wdt_ref[...]) + dtb_ref[...])

    def conv_silu(rb, raw, col0, width):
        cs = slice(col0, col0 + width)
        r0 = SUBLANE + rb * ROW_BLOCK
        if rb == 0:
            cbuf_scr[0:SUBLANE, cs] = jnp.where(first, 0.0, carry_scr[:, cs])
        cbuf_scr[r0:r0 + ROW_BLOCK, cs] = raw
        if rb == nrb - 1:
            carry_scr[:, cs] = raw[ROW_BLOCK - SUBLANE:, :]
        y = cb_ref[:, cs]
        for k in range(SSM_CONV):
            off = r0 - (SSM_CONV - 1) + k
            y = y + cw_ref[k:k + 1, cs] * cbuf_scr[off:off + ROW_BLOCK, cs]
        return _silu(y)

    def qk_norm_rope(rb, t, gain_ref, scale):
        outs = []
        half = ROPE_DIM // 2
        for c in range(t.shape[1] // LANE):
            tc = t[:, c * LANE:(c + 1) * LANE]
            ms = _dot((tc * tc).astype(BF16), bd_ref[...]) * (1.0 / DA_HEAD_DIM)
            tn = tc * lax.rsqrt(ms + EPS) * gain_ref[...]
            rot = (tn * rc_ref[rows[rb], :]
                   + pltpu.roll(tn, LANE - half, 1) * rs1_ref[rows[rb], :]
                   + pltpu.roll(tn, half, 1) * rs2_ref[rows[rb], :])
            outs.append(rot * scale)
        return jnp.concatenate(outs, axis=1)

    @pl.when(i >= 0)
    def _():
        for rb in range(nrb):
            norm_rows(rb)

    def produce(j, rb):
        return _dot_cols(h_scr[rows[rb], :], w_ref, j * TN, TN)

    def consume(unit, acc):
        j, rb = unit
        r = rows[rb]
        lo = slice(j * TN, j * TN + half_w)
        hi = slice(j * TN + half_w, (j + 1) * TN)
        full = slice(j * TN, (j + 1) * TN)
        if j == P_U // TN:
            p_ref[r, lo] = _gelu_tanh(acc[:, :half_w]).astype(BF16)
            gv = _gelu_tanh(acc[:, half_w:])
            vn = gv * lax.rsqrt(jnp.mean(gv * gv, axis=-1, keepdims=True) + EPS) * vng_ref[...]
            p_ref[r, hi] = vn.astype(BF16)
        elif j == P_Z // TN:
            p_ref[r, full] = _silu(acc).astype(BF16)
        elif j == P_XS // TN:
            p_ref[r, full] = conv_silu(rb, acc, 0, TN).astype(BF16)
        elif j == P_BC // TN:
            p_ref[r, lo] = conv_silu(rb, acc[:, :half_w], SSM_INNER, half_w).astype(BF16)
            p_ref[r, hi] = qk_norm_rope(rb, acc[:, half_w:], qg_ref, DA_HEAD_DIM ** -0.5 * LOG2E).astype(BF16)
        elif j == P_K // TN:
            p_ref[r, lo] = qk_norm_rope(rb, acc[:, :half_w], kg_ref, 1.0).astype(BF16)
            v = acc[:, half_w:]
            p_ref[r, hi] = v.astype(BF16)
            vt_ref[0, :, r] = v.T.astype(BF16)
        else:
            p_ref[r, full] = _sigmoid(acc).astype(BF16)

    units = [(j, rb) for j in range(P_WIDTH // TN) for rb in range(nrb)]
    _pipelined(units, produce, consume, depth=2)


def _in_proj(x2d, seq, norm_g, w_in, gm_vg, conv_w, conv_b, dt_bias, q_g, k_g, rope_tabs, bd, *, tm, tk):
    tokens = x2d.shape[0]
    tiles_per_seq = seq // tm
    tiles_per_chunk = tk // tm
    w_main = jnp.concatenate([w_in[:, :DT_OFF], w_in[:, DT_OFF + SSM_HEADS:]], axis=1).astype(BF16)
    w_dt = jnp.pad(w_in[:, DT_OFF:DT_OFF + SSM_HEADS], ((0, 0), (0, LANE - SSM_HEADS))).astype(BF16)
    dtb = jnp.pad(dt_bias, (0, LANE - SSM_HEADS)).reshape(1, LANE)
    rc, rs1, rs2 = rope_tabs
    const = lambda i: (0, 0)
    rope_map = lambda i: (i % tiles_per_seq, 0)
    kern = functools.partial(_in_proj_kernel, tm=tm, tiles_per_seq=tiles_per_seq)
    return pl.pallas_call(
        kern,
        grid=(tokens // tm,),
        in_specs=[
            pl.BlockSpec((tm, D_MODEL), lambda i: (i, 0)),
            pl.BlockSpec((1, D_MODEL), const),
            pl.BlockSpec((D_MODEL, P_WIDTH), const, pipeline_mode=pl.Buffered(1)),
            pl.BlockSpec((D_MODEL, LANE), const),
            pl.BlockSpec((1, LANE), const),
            pl.BlockSpec((1, GM_WIDTH), const),
            pl.BlockSpec((SSM_CONV, SSM_CONV_DIM), const),
            pl.BlockSpec((1, SSM_CONV_DIM), const),
            pl.BlockSpec((1, LANE), const),
            pl.BlockSpec((1, LANE), const),
            pl.BlockSpec((tm, LANE), rope_map),
            pl.BlockSpec((tm, LANE), rope_map),
            pl.BlockSpec((tm, LANE), rope_map),
            pl.BlockSpec((LANE, LANE), const),
        ],
        out_specs=[
            pl.BlockSpec((tm, P_WIDTH), lambda i: (i, 0)),
            pl.BlockSpec((tm, LANE), lambda i: (i, 0)),
            pl.BlockSpec((1, DA_WIDTH, tm), lambda i: (i // tiles_per_chunk, 0, i % tiles_per_chunk)),
        ],
        out_shape=[
            jax.ShapeDtypeStruct((tokens, P_WIDTH), BF16),
            jax.ShapeDtypeStruct((tokens, LANE), F32),
            jax.ShapeDtypeStruct((tokens // tk, DA_WIDTH, tk), BF16),
        ],
        scratch_shapes=[
            pltpu.VMEM((tm, D_MODEL), BF16),
            pltpu.VMEM((tm + SUBLANE, SSM_CONV_DIM), F32),
            pltpu.VMEM((SUBLANE, SSM_CONV_DIM), F32),
        ],
        compiler_params=pltpu.CompilerParams(
            dimension_semantics=("arbitrary",), vmem_limit_bytes=VMEM_LIMIT_BYTES),
        name="in_proj",
    )(x2d, norm_g.reshape(1, D_MODEL), w_main, w_dt, dtb, gm_vg.reshape(1, GM_WIDTH),
      conv_w.T, conv_b.reshape(1, SSM_CONV_DIM),
      jnp.tile(q_g, LANE // DA_HEAD_DIM).reshape(1, LANE), jnp.tile(k_g, LANE // DA_HEAD_DIM).reshape(1, LANE),
      rc, rs1, rs2, bd)


def _rope_tables(seq):
    half = ROPE_DIM // 2
    pos = jnp.arange(seq, dtype=F32)
    inv_freq = 1.0 / (ROPE_THETA ** (jnp.arange(0, ROPE_DIM, 2, dtype=F32) / ROPE_DIM))
    ang = pos[:, None] * inv_freq[None, :]
    cos, sin = jnp.cos(ang), jnp.sin(ang)
    ones = jnp.ones((seq, DA_HEAD_DIM - ROPE_DIM), F32)
    zeros = jnp.zeros((seq, DA_HEAD_DIM - ROPE_DIM), F32)
    zh = jnp.zeros((seq, half), F32)
    rc = jnp.concatenate([cos, cos, ones], axis=1)
    rs1 = jnp.concatenate([-sin, zh, zeros], axis=1)
    rs2 = jnp.concatenate([zh, sin, zeros], axis=1)
    rep = LANE // DA_HEAD_DIM
    return tuple(jnp.tile(t, (1, rep)) for t in (rc, rs1, rs2))


def _ssd_kernel(xs_ref, bc_ref, z_ref, dt_ref, alog_ref, dexp_ref, ng_ref, tril_ref, triu1_ref, e_ref,
                y_ref, state_scr, *, n_chunks):
    q = SSM_CHUNK
    n = SSM_STATE
    gw = SSM_INNER // SSM_GROUPS
    hpg = SSM_HEADS // SSM_GROUPS

    @pl.when(pl.program_id(1) == 0)
    def _():
        state_scr[...] = jnp.zeros_like(state_scr)

    a_row = -jnp.exp(alog_ref[...])
    row = lax.broadcasted_iota(jnp.int32, (q, q), 0)
    col = lax.broadcasted_iota(jnp.int32, (q, q), 1)
    causal = row >= col
    left = col < SSM_HEAD_DIM

    for cc in range(n_chunks):
        rows = slice(cc * q, (cc + 1) * q)
        xs = xs_ref[rows, :]
        dt = dt_ref[rows, :]
        dta = dt * a_row
        acum = _dot_split_left(tril_ref[...], dta, 3)
        ct = _dot_split(dta.T, triu1_ref[...], 3)
        acum_t = ct[:, :q]
        tot_t = ct[:, q:]
        dt_t = dt.T
        w_t = jnp.exp(tot_t - acum_t) * dt_t
        e_exp = _dot_split(jnp.exp(acum), e_ref[...], 2)

        y_parts = []
        st_parts = []
        for g in range(SSM_GROUPS):
            bm = bc_ref[rows, g * n:(g + 1) * n]
            cm = bc_ref[rows, SSM_GROUPS * n + g * n:SSM_GROUPS * n + (g + 1) * n]
            cb = _dot_nt(cm, bm)
            bm_t = bm.astype(F32).T
            prev = state_scr[:, g * gw:(g + 1) * gw].astype(BF16)
            y_parts.append(_dot(cm, prev) * e_exp[:, g * gw:(g + 1) * gw])
            for pair in range(hpg // 2):
                lhs_top = []
                lhs_bot = []
                for hh in range(2):
                    h = g * hpg + 2 * pair + hh
                    seg = jnp.broadcast_to(acum[:, h:h + 1], (q, q)) - acum_t[h:h + 1, :]
                    decay = jnp.exp(jnp.where(causal, seg, -jnp.inf))
                    lhs_top.append((cb * decay * dt_t[h:h + 1, :]).astype(BF16))
                    lhs_bot.append((bm_t * w_t[h:h + 1, :]).astype(BF16))
                lhs = jnp.concatenate(
                    [jnp.concatenate(lhs_top, axis=1), jnp.concatenate(lhs_bot, axis=1)], axis=0)
                c0 = g * gw + pair * LANE
                x2 = xs[:, c0:c0 + LANE]
                zero = jnp.zeros_like(x2)
                rhs = jnp.concatenate([jnp.where(left, x2, zero), jnp.where(left, zero, x2)], axis=0)
                out = _dot(lhs, rhs)
                y_parts.append(out[:q])
                st_parts.append(out[q:])
        y_off = jnp.concatenate([y_parts[0], y_parts[1 + hpg // 2]], axis=1)
        y_diag = jnp.concatenate(y_parts[1:1 + hpg // 2] + y_parts[2 + hpg // 2:], axis=1)
        st = jnp.concatenate(st_parts, axis=1)
        state_scr[...] = state_scr[...] * e_exp[q - 1:q, :] + st
        y = y_diag + y_off + dexp_ref[...] * xs.astype(F32)
        gated = y * z_ref[rows, :].astype(F32)
        out = gated * lax.rsqrt(jnp.mean(gated * gated, axis=-1, keepdims=True) + EPS) * ng_ref[...]
        y_ref[rows, :] = out.astype(BF16)


def _ssd(p, dt, batch, seq, a_log, d_skip, norm_g, consts, *, n_chunks):
    tokens = p.shape[0]
    step = n_chunks * SSM_CHUNK
    steps = seq // step
    tril, triu1, expand = consts
    alog = jnp.pad(a_log, (0, LANE - SSM_HEADS)).reshape(1, LANE)
    dexp = jnp.repeat(d_skip, SSM_HEAD_DIM).reshape(1, SSM_INNER)
    const = lambda b, c: (0, 0)
    return pl.pallas_call(
        functools.partial(_ssd_kernel, n_chunks=n_chunks),
        grid=(batch, steps),
        in_specs=[
            pl.BlockSpec((step, SSM_INNER), lambda b, c: (b * steps + c, P_XS // SSM_INNER)),
            pl.BlockSpec((step, 2 * SSM_GROUPS * SSM_STATE), lambda b, c: (b * steps + c, P_BC // 512)),
            pl.BlockSpec((step, SSM_INNER), lambda b, c: (b * steps + c, P_Z // SSM_INNER)),
            pl.BlockSpec((step, LANE), lambda b, c: (b * steps + c, 0)),
            pl.BlockSpec((1, LANE), const),
            pl.BlockSpec((1, SSM_INNER), const),
            pl.BlockSpec((1, SSM_INNER), const),
            pl.BlockSpec((SSM_CHUNK, SSM_CHUNK), const),
            pl.BlockSpec((SSM_CHUNK, 2 * SSM_CHUNK), const),
            pl.BlockSpec((LANE, SSM_INNER), const),
        ],
        out_specs=pl.BlockSpec((step, SSM_INNER), lambda b, c: (b * steps + c, 0)),
        out_shape=jax.ShapeDtypeStruct((tokens, SSM_INNER), BF16),
        scratch_shapes=[pltpu.VMEM((SSM_STATE, SSM_INNER), F32)],
        compiler_params=pltpu.CompilerParams(
            dimension_semantics=("arbitrary", "arbitrary"), vmem_limit_bytes=VMEM_LIMIT_BYTES),
        name="ssd",
    )(p, p, p, dt, alog, dexp, norm_g.reshape(1, SSM_INNER), tril, triu1, expand)


def _ssd_consts():
    q = SSM_CHUNK
    r = np.arange(q)
    tril = (r[:, None] >= r[None, :]).astype(np.float32)
    triu1 = np.concatenate([tril.T, np.ones((q, q), np.float32)], axis=1)
    expand = np.zeros((LANE, SSM_INNER), np.float32)
    for h in range(SSM_HEADS):
        expand[h, h * SSM_HEAD_DIM:(h + 1) * SSM_HEAD_DIM] = 1.0
    return tuple(jnp.asarray(a, BF16) for a in (tril, triu1, expand))


MAX_STATIC_SHIFT = 40.0


def _diff_attn_kernel(bnd_ref, q_ref, k_ref, vt_ref, lam_ref, g_ref, o_ref, acc_scr, l_scr, *, tq, lam_init):
    qi = pl.program_id(2)
    tk = tq
    q = q_ref[...]
    lane = lax.broadcasted_iota(jnp.int32, q.shape, 1)
    zero = jnp.zeros_like(q)
    qm = (jnp.where(lane < DA_HEAD_DIM, q, zero), jnp.where(lane < DA_HEAD_DIM, zero, q))
    acc_scr[...] = jnp.zeros_like(acc_scr)

    def scores(j, m, masked):
        kc = k_ref[pl.ds(pl.multiple_of(j * tk, tk), tk), :]
        s = _dot_nt(kc, qm[m])
        if masked:
            kk = lax.broadcasted_iota(jnp.int32, s.shape, 0)
            qq = lax.broadcasted_iota(jnp.int32, s.shape, 1)
            s = jnp.where(kk <= qq, s, -jnp.inf)
        return s

    def finalize(l0, l1):
        lv = lam_ref[...]
        lam = (jnp.exp(jnp.sum(lv[0:1] * lv[1:2], axis=-1, keepdims=True))
               - jnp.exp(jnp.sum(lv[2:3] * lv[3:4], axis=-1, keepdims=True)) + lam_init)
        o = acc_scr[0] * (1.0 / l0) - acc_scr[1] * (lam / l1)
        on = o * lax.rsqrt(jnp.mean(o * o, axis=0, keepdims=True) + EPS)
        o_ref[...] = (on.T * g_ref[...] * (1.0 - lam_init)).astype(BF16)

    @pl.when(bnd_ref[1] > 0.5)
    def _():
        shift = bnd_ref[0]
        l_scr[...] = jnp.zeros_like(l_scr)

        def chunk(j, masked):
            vt = vt_ref[j]
            s = [scores(j, m, masked) for m in range(2)]
            for m in range(2):
                p = jnp.exp2(s[m] - shift)
                l_scr[m] += jnp.sum(p.reshape(tk // SUBLANE, SUBLANE, tq), axis=0)
                acc_scr[m] += _dot(vt, p.astype(BF16))

        def body(j, carry):
            chunk(j, False)
            return carry

        lax.fori_loop(0, qi, body, 0)
        chunk(qi, True)
        finalize(jnp.sum(l_scr[0], axis=0, keepdims=True), jnp.sum(l_scr[1], axis=0, keepdims=True))

    @pl.when(bnd_ref[1] <= 0.5)
    def _():
        def chunk(j, stats, masked):
            vt = vt_ref[j]
            new = []
            for m in range(2):
                m_old, l_old = stats[2 * m], stats[2 * m + 1]
                s = scores(j, m, masked)
                m_new = jnp.maximum(m_old, jnp.max(s, axis=0, keepdims=True))
                alpha = jnp.exp2(m_old - m_new)
                p = jnp.exp2(s - m_new)
                acc_scr[m] = acc_scr[m] * alpha + _dot(vt, p.astype(BF16))
                new += [m_new, l_old * alpha + jnp.sum(p, axis=0, keepdims=True)]
            return tuple(new)

        neg = jnp.full((1, tq), -jnp.inf, F32)
        zl = jnp.zeros((1, tq), F32)
        stats = lax.fori_loop(0, qi, lambda j, st: chunk(j, st, False), (neg, zl, neg, zl))
        _, l0, _, l1 = chunk(qi, stats, True)
        finalize(l0, l1)


def _diff_attn(p, vt, batch, seq, q_g, k_g, lam_vecs, subln_g, lam_init, *, tq):
    tokens = p.shape[0]
    nq = seq // tq
    bound = 1.02 * LOG2E * math.sqrt(DA_HEAD_DIM) * jnp.max(jnp.abs(q_g)) * jnp.max(jnp.abs(k_g))
    bnd = jnp.stack([bound, (bound <= MAX_STATIC_SHIFT).astype(F32)]).astype(F32)
    return pl.pallas_call(
        functools.partial(_diff_attn_kernel, tq=tq, lam_init=lam_init),
        grid=(batch, DA_HEADS, nq),
        in_specs=[
            pl.BlockSpec(memory_space=pltpu.SMEM),
            pl.BlockSpec((tq, LANE), lambda b, h, i: (b * nq + i, P_Q // LANE + h)),
            pl.BlockSpec((seq, LANE), lambda b, h, i: (b, P_K // LANE + h)),
            pl.BlockSpec((nq, DA_V_DIM, tq), lambda b, h, i: (b, h, 0)),
            pl.BlockSpec((4, DA_HEAD_DIM), lambda b, h, i: (0, 0)),
            pl.BlockSpec((1, DA_V_DIM), lambda b, h, i: (0, 0)),
        ],
        out_specs=pl.BlockSpec((tq, DA_V_DIM), lambda b, h, i: (b * nq + i, h)),
        out_shape=jax.ShapeDtypeStruct((tokens, DA_WIDTH), BF16),
        scratch_shapes=[
            pltpu.VMEM((2, DA_V_DIM, tq), F32),
            pltpu.VMEM((2, SUBLANE, tq), F32),
        ],
        compiler_params=pltpu.CompilerParams(
            dimension_semantics=("arbitrary", "arbitrary", "arbitrary"), vmem_limit_bytes=VMEM_LIMIT_BYTES),
        name="diff_attn",
    )(bnd, p, p, vt, lam_vecs, subln_g.reshape(1, DA_V_DIM))


def _merge_kernel(u_ref, vn_ref, yb_ref, yc_ref, g0_ref, g1_ref, g2_ref, x_ref,
                  ws_ref, bexp_ref, wa_ref, wb_ref, wc_ref, wo_ref, o_ref, *, tm):
    nch = tm // GM_CHUNK
    gd = GM_WIDTH // GM_GROUPS
    vn = vn_ref[...]
    cols = []
    for g in range(GM_GROUPS):
        rhs = jnp.concatenate([vn[c * GM_CHUNK:(c + 1) * GM_CHUNK, g * gd:(g + 1) * gd] for c in range(nch)], axis=1)
        cols.append(_dot(ws_ref[g], rhs))
    mixed = jnp.concatenate(
        [jnp.concatenate([cols[g][:, c * gd:(c + 1) * gd] for g in range(GM_GROUPS)], axis=1) + bexp_ref[...]
         for c in range(nch)], axis=0)
    ya = (u_ref[...].astype(F32) * mixed).astype(BF16)
    merged = (g0_ref[...].astype(F32) * _dot(ya, wa_ref[...])
              + g1_ref[...].astype(F32) * _dot(yb_ref[...], wb_ref[...])
              + g2_ref[...].astype(F32) * _dot(yc_ref[...], wc_ref[...]))
    o_ref[...] = x_ref[...] + _dot(merged.astype(BF16), wo_ref[...])


def _merge(p, yb, yc, x2d, w_s, b_s, wa, wb, wc, wo, *, tm):
    tokens = x2d.shape[0]
    r = np.arange(GM_CHUNK)
    causal = jnp.asarray(r[:, None] >= r[None, :])
    ws = jnp.where(causal[None], w_s, 0.0).astype(BF16)
    bexp = jnp.repeat(b_s.T, GM_WIDTH // GM_GROUPS, axis=1)
    row = lambda w: pl.BlockSpec((tm, w), lambda i: (i, 0))
    pblk = lambda w, off: pl.BlockSpec((tm, w), lambda i: (i, off // w))
    full = lambda a: pl.BlockSpec(a.shape, lambda i: (0,) * a.ndim)
    args = (ws, bexp, wa.astype(BF16), wb.astype(BF16), wc.astype(BF16), wo.astype(BF16))
    return pl.pallas_call(
        functools.partial(_merge_kernel, tm=tm),
        grid=(tokens // tm,),
        in_specs=[pblk(GM_WIDTH, P_U), pblk(GM_WIDTH, P_V), row(SSM_INNER), row(DA_WIDTH),
                  pblk(D_MODEL, P_G), pblk(D_MODEL, P_G + D_MODEL), pblk(D_MODEL, P_G + 2 * D_MODEL),
                  row(D_MODEL)] + [full(a) for a in args],
        out_specs=row(D_MODEL),
        out_shape=jax.ShapeDtypeStruct((tokens, D_MODEL), F32),
        compiler_params=pltpu.CompilerParams(
            dimension_semantics=("arbitrary",), vmem_limit_bytes=VMEM_LIMIT_BYTES),
        name="merge",
    )(p, p, yb, yc, p, p, p, x2d, *args)


def _ffn_kernel(x_ref, g_ref, wu_ref, cw_ref, cb_ref, wd_ref, o_ref, h_scr, cbuf_scr, carry_scr,
                *, tm, fc, tiles_per_seq):
    i = pl.program_id(0)
    first = (i % tiles_per_seq) == 0
    nrb = tm // ROW_BLOCK
    nc = FFN_DIM // fc
    rows = [slice(rb * ROW_BLOCK, (rb + 1) * ROW_BLOCK) for rb in range(nrb)]

    def norm_rows(rb):
        x = x_ref[rows[rb], :]
        h = x * lax.rsqrt(jnp.mean(x * x, axis=-1, keepdims=True) + EPS) * g_ref[...]
        h_scr[rows[rb], :] = h.astype(BF16)

    def cols(half, c):
        return slice(half * FFN_DIM + c * fc, half * FFN_DIM + (c + 1) * fc)

    def up(rb, c):
        if c == 0:
            norm_rows(rb)
        hb = h_scr[rows[rb], :]
        return tuple(_dot(hb, wu_ref[:, cols(half, c)]) for half in range(2))

    def conv(rb, raw, cs):
        r0 = SUBLANE + rb * ROW_BLOCK
        if rb == 0:
            cbuf_scr[0:SUBLANE, cs] = jnp.where(first, 0.0, carry_scr[:, cs])
        cbuf_scr[r0:r0 + ROW_BLOCK, cs] = raw
        if rb == nrb - 1:
            carry_scr[:, cs] = raw[ROW_BLOCK - SUBLANE:, :]
        y = cb_ref[:, cs]
        for k in range(FFN_CONV):
            off = r0 - (FFN_CONV - 1) + k
            y = y + cw_ref[k:k + 1, cs] * cbuf_scr[off:off + ROW_BLOCK, cs]
        return y

    units = [(rb, c) for rb in range(nrb) for c in range(nc)]
    acc = [None]

    def consume(unit, raw):
        rb, c = unit
        gate = conv(rb, raw[0], cols(0, c))
        val = conv(rb, raw[1], cols(1, c))
        act = (_silu(gate) * val).astype(BF16)
        part = _dot(act, wd_ref[c * fc:(c + 1) * fc, :])
        acc[0] = part if c == 0 else acc[0] + part
        if c == nc - 1:
            o_ref[rows[rb], :] = x_ref[rows[rb], :] + acc[0]

    _pipelined(units, up, consume, depth=2)


def _ffn(x2d, seq, norm_g, w_up, conv_w, conv_b, w_down, *, tm, fc):
    tokens = x2d.shape[0]
    const = lambda i: (0, 0)
    resident = lambda shape: pl.BlockSpec(shape, const, pipeline_mode=pl.Buffered(1))
    return pl.pallas_call(
        functools.partial(_ffn_kernel, tm=tm, fc=fc, tiles_per_seq=seq // tm),
        grid=(tokens // tm,),
        in_specs=[
            pl.BlockSpec((tm, D_MODEL), lambda i: (i, 0)),
            pl.BlockSpec((1, D_MODEL), const),
            resident((D_MODEL, 2 * FFN_DIM)),
            pl.BlockSpec((FFN_CONV, 2 * FFN_DIM), const),
            pl.BlockSpec((1, 2 * FFN_DIM), const),
            resident((FFN_DIM, D_MODEL)),
        ],
        out_specs=pl.BlockSpec((tm, D_MODEL), lambda i: (i, 0)),
        out_shape=jax.ShapeDtypeStruct((tokens, D_MODEL), F32),
        scratch_shapes=[
            pltpu.VMEM((tm, D_MODEL), BF16),
            pltpu.VMEM((tm + SUBLANE, 2 * FFN_DIM), F32),
            pltpu.VMEM((SUBLANE, 2 * FFN_DIM), F32),
        ],
        compiler_params=pltpu.CompilerParams(
            dimension_semantics=("arbitrary",), vmem_limit_bytes=VMEM_LIMIT_BYTES),
        name="ffn",
    )(x2d, norm_g.reshape(1, D_MODEL), w_up.astype(BF16), conv_w.T, conv_b.reshape(1, 2 * FFN_DIM),
      w_down.astype(BF16))


def _tile(seq, want):
    t = min(want, seq)
    assert seq % t == 0, (seq, t)
    return t


def kernel(x, attn_norm_g, w_in, gm_v_norm_g, gm_w_s, gm_b_s, ssm_conv_w, ssm_conv_b, ssm_dt_bias, ssm_a_log, ssm_d, ssm_norm_g, da_q_norm_g, da_k_norm_g, da_lambda, da_subln_g, w_branch_a, w_branch_b, w_branch_c, w_out, ffn_norm_g, ffn_w_up, ffn_conv_w, ffn_conv_b, ffn_w_down):
    batch, seq, _ = x.shape
    depth = w_in.shape[0]
    tokens = batch * seq
    tq = _tile(seq, 1024)
    tm_proj = _tile(seq, 512)
    tm_merge = _tile(seq, 512)
    tm_ffn = _tile(seq, 512)
    rope_tabs = _rope_tables(seq)
    r = np.arange(LANE)
    bd = jnp.asarray((r[:, None] // DA_HEAD_DIM) == (r[None, :] // DA_HEAD_DIM), BF16)
    ssd_consts = _ssd_consts()
    x2d = x.reshape(tokens, D_MODEL)
    for layer in range(depth):
        lam_init = 0.8 - 0.6 * math.exp(-0.3 * layer)
        p, dt, vt = _in_proj(x2d, seq, attn_norm_g[layer], w_in[layer], gm_v_norm_g[layer],
                             ssm_conv_w[layer], ssm_conv_b[layer], ssm_dt_bias[layer],
                             da_q_norm_g[layer], da_k_norm_g[layer], rope_tabs, bd, tm=tm_proj, tk=tq)
        yb = _ssd(p, dt, batch, seq, ssm_a_log[layer], ssm_d[layer], ssm_norm_g[layer], ssd_consts, n_chunks=2)
        yc = _diff_attn(p, vt, batch, seq, da_q_norm_g[layer], da_k_norm_g[layer], da_lambda[layer],
                        da_subln_g[layer], lam_init, tq=tq)
        x2d = _merge(p, yb, yc, x2d, gm_w_s[layer], gm_b_s[layer], w_branch_a[layer], w_branch_b[layer],
                     w_branch_c[layer], w_out[layer], tm=tm_merge)
        x2d = _ffn(x2d, seq, ffn_norm_g[layer], ffn_w_up[layer], ffn_conv_w[layer], ffn_conv_b[layer],
                   ffn_w_down[layer], tm=tm_ffn, fc=256)
    return x2d.reshape(batch, seq, D_MODEL)
```

```python
import functools
import math

import jax
import jax.numpy as jnp
import numpy as np
from jax import lax
from jax.experimental import pallas as pl
from jax.experimental.pallas import tpu as pltpu

F32 = jnp.float32
BF16 = jnp.bfloat16

D_MODEL = 1024
EPS = 1e-6
GM_WIDTH = 512
GM_GROUPS = 4
GM_CHUNK = 128
SSM_INNER = 1024
SSM_HEAD_DIM = 64
SSM_HEADS = SSM_INNER // SSM_HEAD_DIM
SSM_GROUPS = 2
SSM_STATE = 128
SSM_CONV = 4
SSM_CONV_DIM = SSM_INNER + 2 * SSM_GROUPS * SSM_STATE
SSM_CHUNK = 128
DA_HEADS = 4
DA_HEAD_DIM = 64
DA_V_DIM = 2 * DA_HEAD_DIM
DA_QK_WIDTH = DA_HEADS * 2 * DA_HEAD_DIM
DA_WIDTH = DA_HEADS * DA_V_DIM
ROPE_DIM = DA_HEAD_DIM // 4
ROPE_THETA = 500000.0
FFN_DIM = 2816
FFN_CONV = 3
DT_OFF = 2 * GM_WIDTH + SSM_INNER + SSM_CONV_DIM
LOG2E = math.log2(math.e)

LANE = 128
SUBLANE = 8
VMEM_LIMIT_BYTES = 56 * 1024 * 1024

P_WIDTH = 8192
P_U, P_V, P_Z, P_XS, P_BC, P_Q, P_K, P_VV, P_G = 0, 512, 1024, 2048, 3072, 3584, 4096, 4608, 5120
TN = 1024
ROW_BLOCK = 256


def _dot(a, b):
    return jnp.dot(a, b, preferred_element_type=F32)


def _dot_nt(a, b):
    return lax.dot_general(a, b, (((1,), (1,)), ((), ())), preferred_element_type=F32)


def _sigmoid(x):
    return 1.0 / (1.0 + jnp.exp(-x))


def _silu(x):
    return x * _sigmoid(x)


def _gelu_tanh(x):
    c = math.sqrt(2.0 / math.pi)
    return 0.5 * x * (1.0 + jnp.tanh(c * (x + 0.044715 * (x * x * x))))


def _softplus(x):
    return jnp.maximum(x, 0.0) + jnp.log(1.0 + jnp.exp(-jnp.abs(x)))


def _split_bf16(x, parts):
    out = []
    r = x
    for _ in range(parts):
        p = r.astype(BF16)
        out.append(p)
        r = r - p.astype(F32)
    return out


def _dot_split(x, m, parts):
    return sum(_dot(p, m) for p in _split_bf16(x, parts))


def _dot_split_left(m, x, parts):
    return sum(_dot(m, p) for p in _split_bf16(x, parts))


def _pipelined(units, produce, consume, depth):
    queue = [produce(*units[k]) for k in range(min(depth, len(units)))]
    for u, unit in enumerate(units):
        value = queue.pop(0)
        if u + depth < len(units):
            queue.append(produce(*units[u + depth]))
        consume(unit, value)


def _layer_spec(arr, layer, **kwargs):
    shape = arr.shape[1:]
    return pl.BlockSpec((None,) + shape, lambda *_: (layer,) + (0,) * len(shape), **kwargs)


def _in_proj_kernel(x_ref, g_ref, w_ref, wdt_ref, dtb_ref, vng_ref, cw_ref, cb_ref, qg_ref, kg_ref,
                    rc_ref, rs1_ref, rs2_ref, bd_ref,
                    p_ref, dt_ref, vt_ref,
                    h_scr, cbuf_scr, carry_scr, *, tm, tiles_per_seq):
    i = pl.program_id(0)
    first = (i % tiles_per_seq) == 0
    half_w = TN // 2
    nrb = tm // ROW_BLOCK
    rows = [slice(rb * ROW_BLOCK, (rb + 1) * ROW_BLOCK) for rb in range(nrb)]

    def norm_rows(rb):
        x = x_ref[rows[rb], :]
        h = x * lax.rsqrt(jnp.mean(x * x, axis=-1, keepdims=True) + EPS) * g_ref[...]
        hb = h.astype(BF16)
        h_scr[rows[rb], :] = hb
        dt_ref[rows[rb], :] = _softplus(_dot(hb, wdt_ref[...]) + dtb_ref[...])

    def conv_silu(rb, raw, col0, width):
        cs = slice(col0, col0 + width)
        r0 = SUBLANE + rb * ROW_BLOCK
        if rb == 0:
            cbuf_scr[0:SUBLANE, cs] = jnp.where(first, 0.0, carry_scr[:, cs])
        cbuf_scr[r0:r0 + ROW_BLOCK, cs] = raw
        if rb == nrb - 1:
            carry_scr[:, cs] = raw[ROW_BLOCK - SUBLANE:, :]
        ext = cbuf_scr[r0 - SUBLANE:r0 + ROW_BLOCK, cs]
        y = cb_ref[:, cs] + cw_ref[SSM_CONV - 1:SSM_CONV, cs] * raw
        for k in range(SSM_CONV - 1):
            shifted = pltpu.roll(ext, SSM_CONV - 1 - k, 0)[SUBLANE:, :]
            y = y + cw_ref[k:k + 1, cs] * shifted
        return _silu(y)

    def qk_norm_rope(rb, t, gain_ref, scale):
        outs = []
        half = ROPE_DIM // 2
        for c in range(t.shape[1] // LANE):
            tc = t[:, c * LANE:(c + 1) * LANE]
            ms = _dot((tc * tc).astype(BF16), bd_ref[...]) * (1.0 / DA_HEAD_DIM)
            tn = tc * lax.rsqrt(ms + EPS) * gain_ref[...]
            rot = (tn * rc_ref[rows[rb], :]
                   + pltpu.roll(tn, LANE - half, 1) * rs1_ref[rows[rb], :]
                   + pltpu.roll(tn, half, 1) * rs2_ref[rows[rb], :])
            outs.append(rot * scale)
        return jnp.concatenate(outs, axis=1)

    def produce(j, rb):
        if j == 0:
            norm_rows(rb)
        return _dot(h_scr[rows[rb], :], w_ref[:, j * TN:(j + 1) * TN])

    def consume(unit, acc):
        j, rb = unit
        r = rows[rb]
        lo = slice(j * TN, j * TN + half_w)
        hi = slice(j * TN + half_w, (j + 1) * TN)
        full = slice(j * TN, (j + 1) * TN)
        if j == P_U // TN:
            p_ref[r, lo] = _gelu_tanh(acc[:, :half_w]).astype(BF16)
            gv = _gelu_tanh(acc[:, half_w:])
            vn = gv * lax.rsqrt(jnp.mean(gv * gv, axis=-1, keepdims=True) + EPS) * vng_ref[...]
            p_ref[r, hi] = vn.astype(BF16)
        elif j == P_Z // TN:
            p_ref[r, full] = _silu(acc).astype(BF16)
        elif j == P_XS // TN:
            p_ref[r, full] = conv_silu(rb, acc, 0, TN).astype(BF16)
        elif j == P_BC // TN:
            p_ref[r, lo] = conv_silu(rb, acc[:, :half_w], SSM_INNER, half_w).astype(BF16)
            p_ref[r, hi] = qk_norm_rope(rb, acc[:, half_w:], qg_ref, DA_HEAD_DIM ** -0.5 * LOG2E).astype(BF16)
        elif j == P_K // TN:
            p_ref[r, lo] = qk_norm_rope(rb, acc[:, :half_w], kg_ref, 1.0).astype(BF16)
            v = acc[:, half_w:]
            p_ref[r, hi] = v.astype(BF16)
            vt_ref[0, :, r] = v.T.astype(BF16)
        else:
            p_ref[r, full] = _sigmoid(acc).astype(BF16)

    units = [(j, rb) for j in range(P_WIDTH // TN) for rb in range(nrb)]
    _pipelined(units, produce, consume, depth=2)


def _in_proj(x2d, seq, prm, layer, rope_tabs, bd, *, tm, tk):
    tokens = x2d.shape[0]
    tiles_per_seq = seq // tm
    tiles_per_chunk = tk // tm
    rope_map = lambda i: (i % tiles_per_seq, 0)
    params = [prm[k] for k in ("attn_norm_g", "w_main", "w_dt", "dt_bias", "gm_v_norm_g", "ssm_conv_w",
                               "ssm_conv_b", "da_q_norm_g", "da_k_norm_g")]
    param_specs = [_layer_spec(a, layer, pipeline_mode=pl.Buffered(1)) if a is prm["w_main"]
                   else _layer_spec(a, layer) for a in params]
    kern = functools.partial(_in_proj_kernel, tm=tm, tiles_per_seq=tiles_per_seq)
    return pl.pallas_call(
        kern,
        grid=(tokens // tm,),
        in_specs=[pl.BlockSpec((tm, D_MODEL), lambda i: (i, 0))] + param_specs + [
            pl.BlockSpec((tm, LANE), rope_map),
            pl.BlockSpec((tm, LANE), rope_map),
            pl.BlockSpec((tm, LANE), rope_map),
            pl.BlockSpec((LANE, LANE), lambda i: (0, 0)),
        ],
        out_specs=[
            pl.BlockSpec((tm, P_WIDTH), lambda i: (i, 0)),
            pl.BlockSpec((tm, LANE), lambda i: (i, 0)),
            pl.BlockSpec((1, DA_WIDTH, tm), lambda i: (i // tiles_per_chunk, 0, i % tiles_per_chunk)),
        ],
        out_shape=[
            jax.ShapeDtypeStruct((tokens, P_WIDTH), BF16),
            jax.ShapeDtypeStruct((tokens, LANE), F32),
            jax.ShapeDtypeStruct((tokens // tk, DA_WIDTH, tk), BF16),
        ],
        scratch_shapes=[
            pltpu.VMEM((tm, D_MODEL), BF16),
            pltpu.VMEM((tm + SUBLANE, SSM_CONV_DIM), F32),
            pltpu.VMEM((SUBLANE, SSM_CONV_DIM), F32),
        ],
        compiler_params=pltpu.CompilerParams(
            dimension_semantics=("arbitrary",), vmem_limit_bytes=VMEM_LIMIT_BYTES),
        name="in_proj",
    )(x2d, *params, *rope_tabs, bd)


def _rope_tables(seq):
    half = ROPE_DIM // 2
    pos = jnp.arange(seq, dtype=F32)
    inv_freq = 1.0 / (ROPE_THETA ** (jnp.arange(0, ROPE_DIM, 2, dtype=F32) / ROPE_DIM))
    ang = pos[:, None] * inv_freq[None, :]
    cos, sin = jnp.cos(ang), jnp.sin(ang)
    ones = jnp.ones((seq, DA_HEAD_DIM - ROPE_DIM), F32)
    zeros = jnp.zeros((seq, DA_HEAD_DIM - ROPE_DIM), F32)
    zh = jnp.zeros((seq, half), F32)
    rc = jnp.concatenate([cos, cos, ones], axis=1)
    rs1 = jnp.concatenate([-sin, zh, zeros], axis=1)
    rs2 = jnp.concatenate([zh, sin, zeros], axis=1)
    rep = LANE // DA_HEAD_DIM
    return tuple(jnp.tile(t, (1, rep)) for t in (rc, rs1, rs2))


def _ssd_kernel(xs_ref, bc_ref, z_ref, dt_ref, alog_ref, dexp_ref, ng_ref, tril_ref, triu1_ref, e_ref,
                y_ref, state_scr, *, n_chunks):
    q = SSM_CHUNK
    n = SSM_STATE
    gw = SSM_INNER // SSM_GROUPS
    hpg = SSM_HEADS // SSM_GROUPS

    @pl.when(pl.program_id(1) == 0)
    def _():
        state_scr[...] = jnp.zeros_like(state_scr)

    a_row = -jnp.exp(alog_ref[...])
    row = lax.broadcasted_iota(jnp.int32, (q, q), 0)
    col = lax.broadcasted_iota(jnp.int32, (q, q), 1)
    causal = row >= col
    left = col < SSM_HEAD_DIM

    for cc in range(n_chunks):
        rows = slice(cc * q, (cc + 1) * q)
        xs = xs_ref[rows, :]
        dt = dt_ref[rows, :]
        dta = dt * a_row
        acum = _dot_split_left(tril_ref[...], dta, 3)
        ct = _dot_split(dta.T, triu1_ref[...], 3)
        acum_t = ct[:, :q]
        tot_t = ct[:, q:]
        dt_t = dt.T
        w_t = jnp.exp(tot_t - acum_t) * dt_t
        e_exp = _dot_split(jnp.exp(acum), e_ref[...], 2)

        y_parts = []
        st_parts = []
        for g in range(SSM_GROUPS):
            bm = bc_ref[rows, g * n:(g + 1) * n]
            cm = bc_ref[rows, SSM_GROUPS * n + g * n:SSM_GROUPS * n + (g + 1) * n]
            cb = _dot_nt(cm, bm)
            bm_t = bm.astype(F32).T
            prev = state_scr[:, g * gw:(g + 1) * gw].astype(BF16)
            y_parts.append(_dot(cm, prev) * e_exp[:, g * gw:(g + 1) * gw])
            for pair in range(hpg // 2):
                lhs_top = []
                lhs_bot = []
                for hh in range(2):
                    h = g * hpg + 2 * pair + hh
                    seg = jnp.broadcast_to(acum[:, h:h + 1], (q, q)) - acum_t[h:h + 1, :]
                    decay = jnp.exp(jnp.where(causal, seg, -jnp.inf))
                    lhs_top.append((cb * decay * dt_t[h:h + 1, :]).astype(BF16))
                    lhs_bot.append((bm_t * w_t[h:h + 1, :]).astype(BF16))
                lhs = jnp.concatenate(
                    [jnp.concatenate(lhs_top, axis=1), jnp.concatenate(lhs_bot, axis=1)], axis=0)
                c0 = g * gw + pair * LANE
                x2 = xs[:, c0:c0 + LANE]
                zero = jnp.zeros_like(x2)
                rhs = jnp.concatenate([jnp.where(left, x2, zero), jnp.where(left, zero, x2)], axis=0)
                out = _dot(lhs, rhs)
                y_parts.append(out[:q])
                st_parts.append(out[q:])
        y_off = jnp.concatenate([y_parts[0], y_parts[1 + hpg // 2]], axis=1)
        y_diag = jnp.concatenate(y_parts[1:1 + hpg // 2] + y_parts[2 + hpg // 2:], axis=1)
        st = jnp.concatenate(st_parts, axis=1)
        state_scr[...] = state_scr[...] * e_exp[q - 1:q, :] + st
        y = y_diag + y_off + dexp_ref[...] * xs.astype(F32)
        gated = y * z_ref[rows, :].astype(F32)
        out = gated * lax.rsqrt(jnp.mean(gated * gated, axis=-1, keepdims=True) + EPS) * ng_ref[...]
        y_ref[rows, :] = out.astype(BF16)


def _ssd(p, dt, batch, seq, prm, layer, consts, *, n_chunks):
    tokens = p.shape[0]
    step = n_chunks * SSM_CHUNK
    steps = seq // step
    params = [prm[k] for k in ("ssm_a_log", "ssm_d", "ssm_norm_g")]
    return pl.pallas_call(
        functools.partial(_ssd_kernel, n_chunks=n_chunks),
        grid=(batch, steps),
        in_specs=[
            pl.BlockSpec((step, SSM_INNER), lambda b, c: (b * steps + c, P_XS // SSM_INNER)),
            pl.BlockSpec((step, 2 * SSM_GROUPS * SSM_STATE), lambda b, c: (b * steps + c, P_BC // 512)),
            pl.BlockSpec((step, SSM_INNER), lambda b, c: (b * steps + c, P_Z // SSM_INNER)),
            pl.BlockSpec((step, LANE), lambda b, c: (b * steps + c, 0)),
        ] + [_layer_spec(a, layer) for a in params]
        + [pl.BlockSpec(a.shape, lambda b, c: (0, 0)) for a in consts],
        out_specs=pl.BlockSpec((step, SSM_INNER), lambda b, c: (b * steps + c, 0)),
        out_shape=jax.ShapeDtypeStruct((tokens, SSM_INNER), BF16),
        scratch_shapes=[pltpu.VMEM((SSM_STATE, SSM_INNER), F32)],
        compiler_params=pltpu.CompilerParams(
            dimension_semantics=("arbitrary", "arbitrary"), vmem_limit_bytes=VMEM_LIMIT_BYTES),
        name="ssd",
    )(p, p, p, dt, *params, *consts)


def _ssd_consts():
    q = SSM_CHUNK
    r = np.arange(q)
    tril = (r[:, None] >= r[None, :]).astype(np.float32)
    triu1 = np.concatenate([tril.T, np.ones((q, q), np.float32)], axis=1)
    expand = np.zeros((LANE, SSM_INNER), np.float32)
    for h in range(SSM_HEADS):
        expand[h, h * SSM_HEAD_DIM:(h + 1) * SSM_HEAD_DIM] = 1.0
    return tuple(jnp.asarray(a, BF16) for a in (tril, triu1, expand))


MAX_STATIC_SHIFT = 40.0


def _diff_attn_kernel(bnd_ref, q_ref, k_ref, vt_ref, lam_ref, g_ref, o_ref, acc_scr, l_scr,
                      *, tq, lam_init, layer):
    qi = pl.program_id(2)
    tk = tq
    q = q_ref[...]
    lane = lax.broadcasted_iota(jnp.int32, q.shape, 1)
    zero = jnp.zeros_like(q)
    qm = (jnp.where(lane < DA_HEAD_DIM, q, zero), jnp.where(lane < DA_HEAD_DIM, zero, q))
    acc_scr[...] = jnp.zeros_like(acc_scr)

    def scores(j, m, masked):
        kc = k_ref[pl.ds(pl.multiple_of(j * tk, tk), tk), :]
        s = _dot_nt(kc, qm[m])
        if masked:
            kk = lax.broadcasted_iota(jnp.int32, s.shape, 0)
            qq = lax.broadcasted_iota(jnp.int32, s.shape, 1)
            s = jnp.where(kk <= qq, s, -jnp.inf)
        return s

    def finalize(l0, l1):
        lv = lam_ref[...]
        lam = (jnp.exp(jnp.sum(lv[0:1] * lv[1:2], axis=-1, keepdims=True))
               - jnp.exp(jnp.sum(lv[2:3] * lv[3:4], axis=-1, keepdims=True)) + lam_init)
        o = acc_scr[0] * (1.0 / l0) - acc_scr[1] * (lam / l1)
        on = o * lax.rsqrt(jnp.mean(o * o, axis=0, keepdims=True) + EPS)
        o_ref[...] = (on.T * g_ref[...] * (1.0 - lam_init)).astype(BF16)

    @pl.when(bnd_ref[layer, 1] > 0.5)
    def _():
        shift = bnd_ref[layer, 0]
        l_scr[...] = jnp.zeros_like(l_scr)

        def chunk(j, masked):
            vt = vt_ref[j]
            s = [scores(j, m, masked) for m in range(2)]
            for m in range(2):
                p = jnp.exp2(s[m] - shift)
                l_scr[m] += jnp.sum(p.reshape(tk // SUBLANE, SUBLANE, tq), axis=0)
                acc_scr[m] += _dot(vt, p.astype(BF16))

        def body(j, carry):
            chunk(j, False)
            return carry

        lax.fori_loop(0, qi, body, 0)
        chunk(qi, True)
        finalize(jnp.sum(l_scr[0], axis=0, keepdims=True), jnp.sum(l_scr[1], axis=0, keepdims=True))

    @pl.when(bnd_ref[layer, 1] <= 0.5)
    def _():
        def chunk(j, stats, masked):
            vt = vt_ref[j]
            new = []
            for m in range(2):
                m_old, l_old = stats[2 * m], stats[2 * m + 1]
                s = scores(j, m, masked)
                m_new = jnp.maximum(m_old, jnp.max(s, axis=0, keepdims=True))
                alpha = jnp.exp2(m_old - m_new)
                p = jnp.exp2(s - m_new)
                acc_scr[m] = acc_scr[m] * alpha + _dot(vt, p.astype(BF16))
                new += [m_new, l_old * alpha + jnp.sum(p, axis=0, keepdims=True)]
            return tuple(new)

        neg = jnp.full((1, tq), -jnp.inf, F32)
        zl = jnp.zeros((1, tq), F32)
        stats = lax.fori_loop(0, qi, lambda j, st: chunk(j, st, False), (neg, zl, neg, zl))
        _, l0, _, l1 = chunk(qi, stats, True)
        finalize(l0, l1)


def _diff_attn(p, vt, batch, seq, prm, layer, lam_init, *, tq):
    tokens = p.shape[0]
    nq = seq // tq
    params = [prm[k] for k in ("da_lambda", "da_subln_g")]
    return pl.pallas_call(
        functools.partial(_diff_attn_kernel, tq=tq, lam_init=lam_init, layer=layer),
        grid=(batch, DA_HEADS, nq),
        in_specs=[
            pl.BlockSpec(memory_space=pltpu.SMEM),
            pl.BlockSpec((tq, LANE), lambda b, h, i: (b * nq + i, P_Q // LANE + h)),
            pl.BlockSpec((seq, LANE), lambda b, h, i: (b, P_K // LANE + h)),
            pl.BlockSpec((nq, DA_V_DIM, tq), lambda b, h, i: (b, h, 0)),
        ] + [_layer_spec(a, layer) for a in params],
        out_specs=pl.BlockSpec((tq, DA_V_DIM), lambda b, h, i: (b * nq + i, h)),
        out_shape=jax.ShapeDtypeStruct((tokens, DA_WIDTH), BF16),
        scratch_shapes=[
            pltpu.VMEM((2, DA_V_DIM, tq), F32),
            pltpu.VMEM((2, SUBLANE, tq), F32),
        ],
        compiler_params=pltpu.CompilerParams(
            dimension_semantics=("arbitrary", "arbitrary", "arbitrary"), vmem_limit_bytes=VMEM_LIMIT_BYTES),
        name="diff_attn",
    )(prm["attn_shift"], p, p, vt, *params)


def _merge_kernel(u_ref, vn_ref, yb_ref, yc_ref, g0_ref, g1_ref, g2_ref, x_ref,
                  ws_ref, bexp_ref, wa_ref, wb_ref, wc_ref, wo_ref, o_ref, *, tm):
    nch = tm // GM_CHUNK
    gd = GM_WIDTH // GM_GROUPS
    vn = vn_ref[...]
    cols = []
    for g in range(GM_GROUPS):
        rhs = jnp.concatenate([vn[c * GM_CHUNK:(c + 1) * GM_CHUNK, g * gd:(g + 1) * gd] for c in range(nch)], axis=1)
        cols.append(_dot(ws_ref[g], rhs))
    mixed = jnp.concatenate(
        [jnp.concatenate([cols[g][:, c * gd:(c + 1) * gd] for g in range(GM_GROUPS)], axis=1) + bexp_ref[...]
         for c in range(nch)], axis=0)
    ya = (u_ref[...].astype(F32) * mixed).astype(BF16)
    merged = (g0_ref[...].astype(F32) * _dot(ya, wa_ref[...])
              + g1_ref[...].astype(F32) * _dot(yb_ref[...], wb_ref[...])
              + g2_ref[...].astype(F32) * _dot(yc_ref[...], wc_ref[...]))
    o_ref[...] = x_ref[...] + _dot(merged.astype(BF16), wo_ref[...])


def _merge(p, yb, yc, x2d, prm, layer, *, tm):
    tokens = x2d.shape[0]
    row = lambda w: pl.BlockSpec((tm, w), lambda i: (i, 0))
    pblk = lambda w, off: pl.BlockSpec((tm, w), lambda i: (i, off // w))
    params = [prm[k] for k in ("gm_w_s", "gm_b_s", "w_branch_a", "w_branch_b", "w_branch_c", "w_out")]
    return pl.pallas_call(
        functools.partial(_merge_kernel, tm=tm),
        grid=(tokens // tm,),
        in_specs=[pblk(GM_WIDTH, P_U), pblk(GM_WIDTH, P_V), row(SSM_INNER), row(DA_WIDTH),
                  pblk(D_MODEL, P_G), pblk(D_MODEL, P_G + D_MODEL), pblk(D_MODEL, P_G + 2 * D_MODEL),
                  row(D_MODEL)] + [_layer_spec(a, layer) for a in params],
        out_specs=row(D_MODEL),
        out_shape=jax.ShapeDtypeStruct((tokens, D_MODEL), F32),
        compiler_params=pltpu.CompilerParams(
            dimension_semantics=("arbitrary",), vmem_limit_bytes=VMEM_LIMIT_BYTES),
        name="merge",
    )(p, p, yb, yc, p, p, p, x2d, *params)


def _ffn_kernel(x_ref, g_ref, wu_ref, cw_ref, cb_ref, wd_ref, o_ref, h_scr, cbuf_scr, carry_scr,
                *, tm, fc, tiles_per_seq):
    i = pl.program_id(0)
    first = (i % tiles_per_seq) == 0
    nrb = tm // ROW_BLOCK
    nc = FFN_DIM // fc
    rows = [slice(rb * ROW_BLOCK, (rb + 1) * ROW_BLOCK) for rb in range(nrb)]

    def norm_rows(rb):
        x = x_ref[rows[rb], :]
        h = x * lax.rsqrt(jnp.mean(x * x, axis=-1, keepdims=True) + EPS) * g_ref[...]
        h_scr[rows[rb], :] = h.astype(BF16)

    def cols(half, c):
        return slice(half * FFN_DIM + c * fc, half * FFN_DIM + (c + 1) * fc)

    def up(rb, c):
        if c == 0:
            norm_rows(rb)
        hb = h_scr[rows[rb], :]
        return tuple(_dot(hb, wu_ref[:, cols(half, c)]) for half in range(2))

    def conv(rb, raw, cs):
        r0 = SUBLANE + rb * ROW_BLOCK
        if rb == 0:
            cbuf_scr[0:SUBLANE, cs] = jnp.where(first, 0.0, carry_scr[:, cs])
        cbuf_scr[r0:r0 + ROW_BLOCK, cs] = raw
        if rb == nrb - 1:
            carry_scr[:, cs] = raw[ROW_BLOCK - SUBLANE:, :]
        ext = cbuf_scr[r0 - SUBLANE:r0 + ROW_BLOCK, cs]
        y = cb_ref[:, cs] + cw_ref[FFN_CONV - 1:FFN_CONV, cs] * raw
        for k in range(FFN_CONV - 1):
            shifted = pltpu.roll(ext, FFN_CONV - 1 - k, 0)[SUBLANE:, :]
            y = y + cw_ref[k:k + 1, cs] * shifted
        return y

    units = [(rb, c) for rb in range(nrb) for c in range(nc)]
    acc = [None]

    def consume(unit, raw):
        rb, c = unit
        gate = conv(rb, raw[0], cols(0, c))
        val = conv(rb, raw[1], cols(1, c))
        act = (_silu(gate) * val).astype(BF16)
        part = _dot(act, wd_ref[c * fc:(c + 1) * fc, :])
        acc[0] = part if c == 0 else acc[0] + part
        if c == nc - 1:
            o_ref[rows[rb], :] = x_ref[rows[rb], :] + acc[0]

    _pipelined(units, up, consume, depth=2)


def _ffn(x2d, seq, prm, layer, *, tm, fc):
    tokens = x2d.shape[0]
    resident = ("ffn_w_up", "ffn_w_down")
    names = ("ffn_norm_g", "ffn_w_up", "ffn_conv_w", "ffn_conv_b", "ffn_w_down")
    params = [prm[k] for k in names]
    return pl.pallas_call(
        functools.partial(_ffn_kernel, tm=tm, fc=fc, tiles_per_seq=seq // tm),
        grid=(tokens // tm,),
        in_specs=[pl.BlockSpec((tm, D_MODEL), lambda i: (i, 0))] + [
            _layer_spec(prm[k], layer, pipeline_mode=pl.Buffered(1)) if k in resident else _layer_spec(prm[k], layer)
            for k in names],
        out_specs=pl.BlockSpec((tm, D_MODEL), lambda i: (i, 0)),
        out_shape=jax.ShapeDtypeStruct((tokens, D_MODEL), F32),
        scratch_shapes=[
            pltpu.VMEM((tm, D_MODEL), BF16),
            pltpu.VMEM((tm + SUBLANE, 2 * FFN_DIM), F32),
            pltpu.VMEM((SUBLANE, 2 * FFN_DIM), F32),
        ],
        compiler_params=pltpu.CompilerParams(
            dimension_semantics=("arbitrary",), vmem_limit_bytes=VMEM_LIMIT_BYTES),
        name="ffn",
    )(x2d, *params)


def _tile(seq, want):
    t = min(want, seq)
    assert seq % t == 0, (seq, t)
    return t


def _prepare(w):
    depth = w["w_in"].shape[0]
    head_pad = ((0, 0), (0, LANE - SSM_HEADS))
    row = lambda a: a.reshape(depth, 1, a.shape[-1])
    w_in = w["w_in"]
    causal = np.tril(np.ones((GM_CHUNK, GM_CHUNK), bool))
    bound = (1.02 * LOG2E * math.sqrt(DA_HEAD_DIM) * jnp.max(jnp.abs(w["da_q_norm_g"]), axis=-1)
             * jnp.max(jnp.abs(w["da_k_norm_g"]), axis=-1))
    return {
        "attn_norm_g": row(w["attn_norm_g"]),
        "w_main": jnp.concatenate([w_in[:, :, :DT_OFF], w_in[:, :, DT_OFF + SSM_HEADS:]], axis=2).astype(BF16),
        "w_dt": jnp.pad(w_in[:, :, DT_OFF:DT_OFF + SSM_HEADS], ((0, 0),) + head_pad).astype(BF16),
        "dt_bias": row(jnp.pad(w["ssm_dt_bias"], head_pad)),
        "gm_v_norm_g": row(w["gm_v_norm_g"]),
        "ssm_conv_w": jnp.swapaxes(w["ssm_conv_w"], 1, 2),
        "ssm_conv_b": row(w["ssm_conv_b"]),
        "da_q_norm_g": row(jnp.tile(w["da_q_norm_g"], (1, LANE // DA_HEAD_DIM))),
        "da_k_norm_g": row(jnp.tile(w["da_k_norm_g"], (1, LANE // DA_HEAD_DIM))),
        "ssm_a_log": row(jnp.pad(w["ssm_a_log"], head_pad)),
        "ssm_d": row(jnp.repeat(w["ssm_d"], SSM_HEAD_DIM, axis=1)),
        "ssm_norm_g": row(w["ssm_norm_g"]),
        "attn_shift": jnp.stack([bound, (bound <= MAX_STATIC_SHIFT).astype(F32)], axis=1).astype(F32),
        "da_lambda": w["da_lambda"],
        "da_subln_g": row(w["da_subln_g"]),
        "gm_w_s": jnp.where(causal, w["gm_w_s"], 0.0).astype(BF16),
        "gm_b_s": jnp.repeat(jnp.swapaxes(w["gm_b_s"], 1, 2), GM_WIDTH // GM_GROUPS, axis=2),
        "w_branch_a": w["w_branch_a"].astype(BF16),
        "w_branch_b": w["w_branch_b"].astype(BF16),
        "w_branch_c": w["w_branch_c"].astype(BF16),
        "w_out": w["w_out"].astype(BF16),
        "ffn_norm_g": row(w["ffn_norm_g"]),
        "ffn_w_up": w["ffn_w_up"].astype(BF16),
        "ffn_conv_w": jnp.swapaxes(w["ffn_conv_w"], 1, 2),
        "ffn_conv_b": row(w["ffn_conv_b"]),
        "ffn_w_down": w["ffn_w_down"].astype(BF16),
    }


def kernel(x, attn_norm_g, w_in, gm_v_norm_g, gm_w_s, gm_b_s, ssm_conv_w, ssm_conv_b, ssm_dt_bias, ssm_a_log, ssm_d, ssm_norm_g, da_q_norm_g, da_k_norm_g, da_lambda, da_subln_g, w_branch_a, w_branch_b, w_branch_c, w_out, ffn_norm_g, ffn_w_up, ffn_conv_w, ffn_conv_b, ffn_w_down):
    batch, seq, _ = x.shape
    depth = w_in.shape[0]
    tokens = batch * seq
    tq = _tile(seq, 1024)
    tm_proj = _tile(seq, 512)
    tm_merge = _tile(seq, 512)
    tm_ffn = _tile(seq, 512)
    prm = _prepare(dict(
        attn_norm_g=attn_norm_g, w_in=w_in, gm_v_norm_g=gm_v_norm_g, gm_w_s=gm_w_s, gm_b_s=gm_b_s,
        ssm_conv_w=ssm_conv_w, ssm_conv_b=ssm_conv_b, ssm_dt_bias=ssm_dt_bias, ssm_a_log=ssm_a_log, ssm_d=ssm_d,
        ssm_norm_g=ssm_norm_g, da_q_norm_g=da_q_norm_g, da_k_norm_g=da_k_norm_g, da_lambda=da_lambda,
        da_subln_g=da_subln_g, w_branch_a=w_branch_a, w_branch_b=w_branch_b, w_branch_c=w_branch_c, w_out=w_out,
        ffn_norm_g=ffn_norm_g, ffn_w_up=ffn_w_up, ffn_conv_w=ffn_conv_w, ffn_conv_b=ffn_conv_b,
        ffn_w_down=ffn_w_down))
    rope_tabs = _rope_tables(seq)
    r = np.arange(LANE)
    bd = jnp.asarray((r[:, None] // DA_HEAD_DIM) == (r[None, :] // DA_HEAD_DIM), BF16)
    ssd_consts = _ssd_consts()
    x2d = x.reshape(tokens, D_MODEL)
    for layer in range(depth):
        lam_init = 0.8 - 0.6 * math.exp(-0.3 * layer)
        p, dt, vt = _in_proj(x2d, seq, prm, layer, rope_tabs, bd, tm=tm_proj, tk=tq)
        yb = _ssd(p, dt, batch, seq, prm, layer, ssd_consts, n_chunks=2)
        yc = _diff_attn(p, vt, batch, seq, prm, layer, lam_init, tq=tq)
        x2d = _merge(p, yb, yc, x2d, prm, layer, tm=tm_merge)
        x2d = _ffn(x2d, seq, prm, layer, tm=tm_ffn, fc=256)
    return x2d.reshape(batch, seq, D_MODEL)
```

```python
import functools
import math

import jax
import jax.numpy as jnp
import numpy as np
from jax import lax
from jax.experimental import pallas as pl
from jax.experimental.pallas import tpu as pltpu

F32 = jnp.float32
BF16 = jnp.bfloat16

D_MODEL = 1024
EPS = 1e-6
GM_WIDTH = 512
GM_GROUPS = 4
GM_CHUNK = 128
SSM_INNER = 1024
SSM_HEAD_DIM = 64
SSM_HEADS = SSM_INNER // SSM_HEAD_DIM
SSM_GROUPS = 2
SSM_STATE = 128
SSM_CONV = 4
SSM_CONV_DIM = SSM_INNER + 2 * SSM_GROUPS * SSM_STATE
SSM_CHUNK = 128
DA_HEADS = 4
DA_HEAD_DIM = 64
DA_V_DIM = 2 * DA_HEAD_DIM
DA_QK_WIDTH = DA_HEADS * 2 * DA_HEAD_DIM
DA_WIDTH = DA_HEADS * DA_V_DIM
ROPE_DIM = DA_HEAD_DIM // 4
ROPE_THETA = 500000.0
FFN_DIM = 2816
FFN_CONV = 3
DT_OFF = 2 * GM_WIDTH + SSM_INNER + SSM_CONV_DIM
LOG2E = math.log2(math.e)

LANE = 128
SUBLANE = 8
VMEM_LIMIT_BYTES = 56 * 1024 * 1024

P_WIDTH = 8192
P_U, P_V, P_Z, P_XS, P_BC, P_Q, P_K, P_VV, P_G = 0, 512, 1024, 2048, 3072, 3584, 4096, 4608, 5120
TN = 512
ROW_BLOCK = 256
PIECE = 256


def _dot(a, b):
    return jnp.dot(a, b, preferred_element_type=F32)


def _dot_nt(a, b):
    return lax.dot_general(a, b, (((1,), (1,)), ((), ())), preferred_element_type=F32)


def _sigmoid(x):
    return 1.0 / (1.0 + jnp.exp(-x))


def _silu(x):
    return x * _sigmoid(x)


def _gelu_tanh(x):
    c = math.sqrt(2.0 / math.pi)
    return 0.5 * x * (1.0 + jnp.tanh(c * (x + 0.044715 * (x * x * x))))


def _softplus(x):
    return jnp.maximum(x, 0.0) + jnp.log(1.0 + jnp.exp(-jnp.abs(x)))


def _split_bf16(x, parts):
    out = []
    r = x
    for _ in range(parts):
        p = r.astype(BF16)
        out.append(p)
        r = r - p.astype(F32)
    return out


def _dot_split(x, m, parts):
    return sum(_dot(p, m) for p in _split_bf16(x, parts))


def _dot_split_left(m, x, parts):
    return sum(_dot(m, p) for p in _split_bf16(x, parts))


def _pipelined(units, produce, consume, depth):
    queue = [produce(*units[k]) for k in range(min(depth, len(units)))]
    for u, unit in enumerate(units):
        value = queue.pop(0)
        if u + depth < len(units):
            queue.append(produce(*units[u + depth]))
        consume(unit, value)


def _layer_spec(arr, layer, **kwargs):
    shape = arr.shape[1:]
    return pl.BlockSpec((None,) + shape, lambda *_: (layer,) + (0,) * len(shape), **kwargs)


def _in_proj_kernel(x_ref, g_ref, w_ref, wdt_ref, dtb_ref, vng_ref, cw_ref, cb_ref, qg_ref, kg_ref,
                    rc_ref, rs1_ref, rs2_ref, bd_ref,
                    p_ref, dt_ref, vt_ref,
                    h_scr, cbuf_scr, carry_scr, *, tm, tiles_per_seq):
    i = pl.program_id(0)
    first = (i % tiles_per_seq) == 0
    nrb = tm // ROW_BLOCK
    rows = [slice(rb * ROW_BLOCK, (rb + 1) * ROW_BLOCK) for rb in range(nrb)]

    def norm_rows(rb):
        x = x_ref[rows[rb], :]
        h = x * lax.rsqrt(jnp.mean(x * x, axis=-1, keepdims=True) + EPS) * g_ref[...]
        hb = h.astype(BF16)
        h_scr[rows[rb], :] = hb
        dt_ref[rows[rb], :] = _softplus(_dot(hb, wdt_ref[...]) + dtb_ref[...])

    def conv_silu(rb, raw, col0, width):
        cs = slice(col0, col0 + width)
        r0 = SUBLANE + rb * ROW_BLOCK
        if rb == 0:
            cbuf_scr[0:SUBLANE, cs] = jnp.where(first, 0.0, carry_scr[:, cs])
        cbuf_scr[r0:r0 + ROW_BLOCK, cs] = raw
        if rb == nrb - 1:
            carry_scr[:, cs] = raw[ROW_BLOCK - SUBLANE:, :]
        ext = cbuf_scr[r0 - SUBLANE:r0 + ROW_BLOCK, cs]
        y = cb_ref[:, cs] + cw_ref[SSM_CONV - 1:SSM_CONV, cs] * raw
        for k in range(SSM_CONV - 1):
            shifted = pltpu.roll(ext, SSM_CONV - 1 - k, 0)[SUBLANE:, :]
            y = y + cw_ref[k:k + 1, cs] * shifted
        return _silu(y)

    def qk_norm_rope(rb, t, gain_ref, scale):
        outs = []
        half = ROPE_DIM // 2
        for c in range(t.shape[1] // LANE):
            tc = t[:, c * LANE:(c + 1) * LANE]
            ms = _dot((tc * tc).astype(BF16), bd_ref[...]) * (1.0 / DA_HEAD_DIM)
            tn = tc * lax.rsqrt(ms + EPS) * gain_ref[...]
            rot = (tn * rc_ref[rows[rb], :]
                   + pltpu.roll(tn, LANE - half, 1) * rs1_ref[rows[rb], :]
                   + pltpu.roll(tn, half, 1) * rs2_ref[rows[rb], :])
            outs.append(rot * scale)
        return jnp.concatenate(outs, axis=1)

    def produce(j, rb):
        if j == 0:
            norm_rows(rb)
        hb = h_scr[rows[rb], :]
        return [_dot(hb, w_ref[:, c:c + PIECE]) for c in range(j * TN, (j + 1) * TN, PIECE)]

    def consume(unit, pieces):
        j, rb = unit
        r = rows[rb]
        gv = []
        for q, acc in enumerate(pieces):
            c0 = j * TN + q * PIECE
            cs = slice(c0, c0 + PIECE)
            if c0 < P_V:
                p_ref[r, cs] = _gelu_tanh(acc).astype(BF16)
            elif c0 < P_Z:
                gv.append((cs, _gelu_tanh(acc)))
            elif c0 < P_XS:
                p_ref[r, cs] = _silu(acc).astype(BF16)
            elif c0 < P_Q:
                p_ref[r, cs] = conv_silu(rb, acc, c0 - P_XS, PIECE).astype(BF16)
            elif c0 < P_K:
                p_ref[r, cs] = qk_norm_rope(rb, acc, qg_ref, DA_HEAD_DIM ** -0.5 * LOG2E).astype(BF16)
            elif c0 < P_VV:
                p_ref[r, cs] = qk_norm_rope(rb, acc, kg_ref, 1.0).astype(BF16)
            elif c0 < P_G:
                p_ref[r, cs] = acc.astype(BF16)
                vt_ref[0, c0 - P_VV:c0 - P_VV + PIECE, r] = acc.T.astype(BF16)
            else:
                p_ref[r, cs] = _sigmoid(acc).astype(BF16)
        if gv:
            ms = sum(jnp.sum(g * g, axis=-1, keepdims=True) for _, g in gv) * (1.0 / GM_WIDTH)
            scale = lax.rsqrt(ms + EPS)
            for cs, g in gv:
                p_ref[r, cs] = (g * scale * vng_ref[:, cs.start - P_V:cs.stop - P_V]).astype(BF16)

    units = [(j, rb) for j in range(P_WIDTH // TN) for rb in range(nrb)]
    _pipelined(units, produce, consume, depth=2)


def _in_proj(x2d, seq, prm, layer, rope_tabs, bd, *, tm, tk):
    tokens = x2d.shape[0]
    tiles_per_seq = seq // tm
    tiles_per_chunk = tk // tm
    rope_map = lambda i: (i % tiles_per_seq, 0)
    params = [prm[k] for k in ("attn_norm_g", "w_main", "w_dt", "dt_bias", "gm_v_norm_g", "ssm_conv_w",
                               "ssm_conv_b", "da_q_norm_g", "da_k_norm_g")]
    param_specs = [_layer_spec(a, layer, pipeline_mode=pl.Buffered(1)) if a is prm["w_main"]
                   else _layer_spec(a, layer) for a in params]
    kern = functools.partial(_in_proj_kernel, tm=tm, tiles_per_seq=tiles_per_seq)
    return pl.pallas_call(
        kern,
        grid=(tokens // tm,),
        in_specs=[pl.BlockSpec((tm, D_MODEL), lambda i: (i, 0))] + param_specs + [
            pl.BlockSpec((tm, LANE), rope_map),
            pl.BlockSpec((tm, LANE), rope_map),
            pl.BlockSpec((tm, LANE), rope_map),
            pl.BlockSpec((LANE, LANE), lambda i: (0, 0)),
        ],
        out_specs=[
            pl.BlockSpec((tm, P_WIDTH), lambda i: (i, 0)),
            pl.BlockSpec((tm, LANE), lambda i: (i, 0)),
            pl.BlockSpec((1, DA_WIDTH, tm), lambda i: (i // tiles_per_chunk, 0, i % tiles_per_chunk)),
        ],
        out_shape=[
            jax.ShapeDtypeStruct((tokens, P_WIDTH), BF16),
            jax.ShapeDtypeStruct((tokens, LANE), F32),
            jax.ShapeDtypeStruct((tokens // tk, DA_WIDTH, tk), BF16),
        ],
        scratch_shapes=[
            pltpu.VMEM((tm, D_MODEL), BF16),
            pltpu.VMEM((tm + SUBLANE, SSM_CONV_DIM), F32),
            pltpu.VMEM((SUBLANE, SSM_CONV_DIM), F32),
        ],
        compiler_params=pltpu.CompilerParams(
            dimension_semantics=("arbitrary",), vmem_limit_bytes=VMEM_LIMIT_BYTES),
        name="in_proj",
    )(x2d, *params, *rope_tabs, bd)


def _rope_tables(seq):
    half = ROPE_DIM // 2
    pos = jnp.arange(seq, dtype=F32)
    inv_freq = 1.0 / (ROPE_THETA ** (jnp.arange(0, ROPE_DIM, 2, dtype=F32) / ROPE_DIM))
    ang = pos[:, None] * inv_freq[None, :]
    cos, sin = jnp.cos(ang), jnp.sin(ang)
    ones = jnp.ones((seq, DA_HEAD_DIM - ROPE_DIM), F32)
    zeros = jnp.zeros((seq, DA_HEAD_DIM - ROPE_DIM), F32)
    zh = jnp.zeros((seq, half), F32)
    rc = jnp.concatenate([cos, cos, ones], axis=1)
    rs1 = jnp.concatenate([-sin, zh, zeros], axis=1)
    rs2 = jnp.concatenate([zh, sin, zeros], axis=1)
    rep = LANE // DA_HEAD_DIM
    return tuple(jnp.tile(t, (1, rep)) for t in (rc, rs1, rs2))


def _ssd_kernel(xs_ref, bc_ref, z_ref, dt_ref, alog_ref, dexp_ref, ng_ref, tril_ref, triu1_ref, e_ref,
                y_ref, state_scr, *, n_chunks):
    q = SSM_CHUNK
    n = SSM_STATE
    gw = SSM_INNER // SSM_GROUPS
    hpg = SSM_HEADS // SSM_GROUPS

    @pl.when(pl.program_id(1) == 0)
    def _():
        state_scr[...] = jnp.zeros_like(state_scr)

    a_row = -jnp.exp(alog_ref[...])
    row = lax.broadcasted_iota(jnp.int32, (q, q), 0)
    col = lax.broadcasted_iota(jnp.int32, (q, q), 1)
    causal = row >= col
    left = col < SSM_HEAD_DIM

    for cc in range(n_chunks):
        rows = slice(cc * q, (cc + 1) * q)
        xs = xs_ref[rows, :]
        dt = dt_ref[rows, :]
        dta = dt * a_row
        acum = _dot_split_left(tril_ref[...], dta, 3)
        ct = _dot_split(dta.T, triu1_ref[...], 3)
        acum_t = ct[:, :q]
        tot_t = ct[:, q:]
        dt_t = dt.T
        w_t = jnp.exp(tot_t - acum_t) * dt_t
        e_exp = _dot_split(jnp.exp(acum), e_ref[...], 2)

        y_parts = []
        st_parts = []
        for g in range(SSM_GROUPS):
            bm = bc_ref[rows, g * n:(g + 1) * n]
            cm = bc_ref[rows, SSM_GROUPS * n + g * n:SSM_GROUPS * n + (g + 1) * n]
            cb = _dot_nt(cm, bm)
            bm_t = bm.astype(F32).T
            prev = state_scr[:, g * gw:(g + 1) * gw].astype(BF16)
            y_parts.append(_dot(cm, prev) * e_exp[:, g * gw:(g + 1) * gw])
            for pair in range(hpg // 2):
                lhs_top = []
                lhs_bot = []
                for hh in range(2):
                    h = g * hpg + 2 * pair + hh
                    seg = jnp.broadcast_to(acum[:, h:h + 1], (q, q)) - acum_t[h:h + 1, :]
                    decay = jnp.exp(jnp.where(causal, seg, -jnp.inf))
                    lhs_top.append((cb * decay * dt_t[h:h + 1, :]).astype(BF16))
                    lhs_bot.append((bm_t * w_t[h:h + 1, :]).astype(BF16))
                lhs = jnp.concatenate(
                    [jnp.concatenate(lhs_top, axis=1), jnp.concatenate(lhs_bot, axis=1)], axis=0)
                c0 = g * gw + pair * LANE
                x2 = xs[:, c0:c0 + LANE]
                zero = jnp.zeros_like(x2)
                rhs = jnp.concatenate([jnp.where(left, x2, zero), jnp.where(left, zero, x2)], axis=0)
                out = _dot(lhs, rhs)
                y_parts.append(out[:q])
                st_parts.append(out[q:])
        y_off = jnp.concatenate([y_parts[0], y_parts[1 + hpg // 2]], axis=1)
        y_diag = jnp.concatenate(y_parts[1:1 + hpg // 2] + y_parts[2 + hpg // 2:], axis=1)
        st = jnp.concatenate(st_parts, axis=1)
        state_scr[...] = state_scr[...] * e_exp[q - 1:q, :] + st
        y = y_diag + y_off + dexp_ref[...] * xs.astype(F32)
        gated = y * z_ref[rows, :].astype(F32)
        out = gated * lax.rsqrt(jnp.mean(gated * gated, axis=-1, keepdims=True) + EPS) * ng_ref[...]
        y_ref[rows, :] = out.astype(BF16)


def _ssd(p, dt, batch, seq, prm, layer, consts, *, n_chunks):
    tokens = p.shape[0]
    step = n_chunks * SSM_CHUNK
    steps = seq // step
    params = [prm[k] for k in ("ssm_a_log", "ssm_d", "ssm_norm_g")]
    return pl.pallas_call(
        functools.partial(_ssd_kernel, n_chunks=n_chunks),
        grid=(batch, steps),
        in_specs=[
            pl.BlockSpec((step, SSM_INNER), lambda b, c: (b * steps + c, P_XS // SSM_INNER)),
            pl.BlockSpec((step, 2 * SSM_GROUPS * SSM_STATE), lambda b, c: (b * steps + c, P_BC // 512)),
            pl.BlockSpec((step, SSM_INNER), lambda b, c: (b * steps + c, P_Z // SSM_INNER)),
            pl.BlockSpec((step, LANE), lambda b, c: (b * steps + c, 0)),
        ] + [_layer_spec(a, layer) for a in params]
        + [pl.BlockSpec(a.shape, lambda b, c: (0, 0)) for a in consts],
        out_specs=pl.BlockSpec((step, SSM_INNER), lambda b, c: (b * steps + c, 0)),
        out_shape=jax.ShapeDtypeStruct((tokens, SSM_INNER), BF16),
        scratch_shapes=[pltpu.VMEM((SSM_STATE, SSM_INNER), F32)],
        compiler_params=pltpu.CompilerParams(
            dimension_semantics=("arbitrary", "arbitrary"), vmem_limit_bytes=VMEM_LIMIT_BYTES),
        name="ssd",
    )(p, p, p, dt, *params, *consts)


def _ssd_consts():
    q = SSM_CHUNK
    r = np.arange(q)
    tril = (r[:, None] >= r[None, :]).astype(np.float32)
    triu1 = np.concatenate([tril.T, np.ones((q, q), np.float32)], axis=1)
    expand = np.zeros((LANE, SSM_INNER), np.float32)
    for h in range(SSM_HEADS):
        expand[h, h * SSM_HEAD_DIM:(h + 1) * SSM_HEAD_DIM] = 1.0
    return tuple(jnp.asarray(a, BF16) for a in (tril, triu1, expand))


MAX_STATIC_SHIFT = 40.0


def _diff_attn_kernel(bnd_ref, q_ref, k_ref, vt_ref, lam_ref, g_ref, o_ref, acc_scr, l_scr,
                      *, tq, lam_init, layer):
    qi = pl.program_id(2)
    tk = tq
    q = q_ref[...]
    lane = lax.broadcasted_iota(jnp.int32, q.shape, 1)
    zero = jnp.zeros_like(q)
    qm = (jnp.where(lane < DA_HEAD_DIM, q, zero), jnp.where(lane < DA_HEAD_DIM, zero, q))
    acc_scr[...] = jnp.zeros_like(acc_scr)

    def scores(j, m, masked):
        kc = k_ref[pl.ds(pl.multiple_of(j * tk, tk), tk), :]
        s = _dot_nt(kc, qm[m])
        if masked:
            kk = lax.broadcasted_iota(jnp.int32, s.shape, 0)
            qq = lax.broadcasted_iota(jnp.int32, s.shape, 1)
            s = jnp.where(kk <= qq, s, -jnp.inf)
        return s

    def finalize(l0, l1):
        lv = lam_ref[...]
        lam = (jnp.exp(jnp.sum(lv[0:1] * lv[1:2], axis=-1, keepdims=True))
               - jnp.exp(jnp.sum(lv[2:3] * lv[3:4], axis=-1, keepdims=True)) + lam_init)
        o = acc_scr[0] * (1.0 / l0) - acc_scr[1] * (lam / l1)
        on = o * lax.rsqrt(jnp.mean(o * o, axis=0, keepdims=True) + EPS)
        o_ref[...] = (on.T * g_ref[...] * (1.0 - lam_init)).astype(BF16)

    @pl.when(bnd_ref[layer, 1] > 0.5)
    def _():
        shift = bnd_ref[layer, 0]
        l_scr[...] = jnp.zeros_like(l_scr)

        def chunk(j, masked):
            vt = vt_ref[j]
            s = [scores(j, m, masked) for m in range(2)]
            for m in range(2):
                p = jnp.exp2(s[m] - shift)
                l_scr[m] += jnp.sum(p.reshape(tk // SUBLANE, SUBLANE, tq), axis=0)
                acc_scr[m] += _dot(vt, p.astype(BF16))

        def body(j, carry):
            chunk(j, False)
            return carry

        lax.fori_loop(0, qi, body, 0)

        half = tk // 2
        vt = vt_ref[qi]
        for k0, q0 in ((0, 0), (half, half)):
            kc = k_ref[pl.ds(pl.multiple_of(qi * tk + k0, half), half), :]
            s = [_dot_nt(kc, qm[m][q0:, :]) for m in range(2)]
            kk = lax.broadcasted_iota(jnp.int32, s[0].shape, 0) + k0
            qq = lax.broadcasted_iota(jnp.int32, s[0].shape, 1) + q0
            for m in range(2):
                p = jnp.exp2(jnp.where(kk <= qq, s[m], -jnp.inf) - shift)
                l_scr[m, :, q0:] += jnp.sum(p.reshape(half // SUBLANE, SUBLANE, tq - q0), axis=0)
                acc_scr[m, :, q0:] += _dot(vt[:, k0:k0 + half], p.astype(BF16))
        finalize(jnp.sum(l_scr[0], axis=0, keepdims=True), jnp.sum(l_scr[1], axis=0, keepdims=True))

    @pl.when(bnd_ref[layer, 1] <= 0.5)
    def _():
        def chunk(j, stats, masked):
            vt = vt_ref[j]
            new = []
            for m in range(2):
                m_old, l_old = stats[2 * m], stats[2 * m + 1]
                s = scores(j, m, masked)
                m_new = jnp.maximum(m_old, jnp.max(s, axis=0, keepdims=True))
                alpha = jnp.exp2(m_old - m_new)
                p = jnp.exp2(s - m_new)
                acc_scr[m] = acc_scr[m] * alpha + _dot(vt, p.astype(BF16))
                new += [m_new, l_old * alpha + jnp.sum(p, axis=0, keepdims=True)]
            return tuple(new)

        neg = jnp.full((1, tq), -jnp.inf, F32)
        zl = jnp.zeros((1, tq), F32)
        stats = lax.fori_loop(0, qi, lambda j, st: chunk(j, st, False), (neg, zl, neg, zl))
        _, l0, _, l1 = chunk(qi, stats, True)
        finalize(l0, l1)


def _diff_attn(p, vt, batch, seq, prm, layer, lam_init, *, tq):
    tokens = p.shape[0]
    nq = seq // tq
    params = [prm[k] for k in ("da_lambda", "da_subln_g")]
    return pl.pallas_call(
        functools.partial(_diff_attn_kernel, tq=tq, lam_init=lam_init, layer=layer),
        grid=(batch, DA_HEADS, nq),
        in_specs=[
            pl.BlockSpec(memory_space=pltpu.SMEM),
            pl.BlockSpec((tq, LANE), lambda b, h, i: (b * nq + i, P_Q // LANE + h)),
            pl.BlockSpec((seq, LANE), lambda b, h, i: (b, P_K // LANE + h)),
            pl.BlockSpec((nq, DA_V_DIM, tq), lambda b, h, i: (b, h, 0)),
        ] + [_layer_spec(a, layer) for a in params],
        out_specs=pl.BlockSpec((tq, DA_V_DIM), lambda b, h, i: (b * nq + i, h)),
        out_shape=jax.ShapeDtypeStruct((tokens, DA_WIDTH), BF16),
        scratch_shapes=[
            pltpu.VMEM((2, DA_V_DIM, tq), F32),
            pltpu.VMEM((2, SUBLANE, tq), F32),
        ],
        compiler_params=pltpu.CompilerParams(
            dimension_semantics=("arbitrary", "arbitrary", "arbitrary"), vmem_limit_bytes=VMEM_LIMIT_BYTES),
        name="diff_attn",
    )(prm["attn_shift"], p, p, vt, *params)


def _merge_kernel(u_ref, vn_ref, yb_ref, yc_ref, g0_ref, g1_ref, g2_ref, x_ref,
                  ws_ref, bexp_ref, wa_ref, wb_ref, wc_ref, wo_ref, o_ref, *, tm):
    nch = tm // GM_CHUNK
    gd = GM_WIDTH // GM_GROUPS
    vn = vn_ref[...]
    cols = []
    for g in range(GM_GROUPS):
        rhs = jnp.concatenate([vn[c * GM_CHUNK:(c + 1) * GM_CHUNK, g * gd:(g + 1) * gd] for c in range(nch)], axis=1)
        cols.append(_dot(ws_ref[g], rhs))
    mixed = jnp.concatenate(
        [jnp.concatenate([cols[g][:, c * gd:(c + 1) * gd] for g in range(GM_GROUPS)], axis=1) + bexp_ref[...]
         for c in range(nch)], axis=0)
    ya = (u_ref[...].astype(F32) * mixed).astype(BF16)
    merged = (g0_ref[...].astype(F32) * _dot(ya, wa_ref[...])
              + g1_ref[...].astype(F32) * _dot(yb_ref[...], wb_ref[...])
              + g2_ref[...].astype(F32) * _dot(yc_ref[...], wc_ref[...]))
    o_ref[...] = x_ref[...] + _dot(merged.astype(BF16), wo_ref[...])


def _merge(p, yb, yc, x2d, prm, layer, *, tm):
    tokens = x2d.shape[0]
    row = lambda w: pl.BlockSpec((tm, w), lambda i: (i, 0))
    pblk = lambda w, off: pl.BlockSpec((tm, w), lambda i: (i, off // w))
    params = [prm[k] for k in ("gm_w_s", "gm_b_s", "w_branch_a", "w_branch_b", "w_branch_c", "w_out")]
    return pl.pallas_call(
        functools.partial(_merge_kernel, tm=tm),
        grid=(tokens // tm,),
        in_specs=[pblk(GM_WIDTH, P_U), pblk(GM_WIDTH, P_V), row(SSM_INNER), row(DA_WIDTH),
                  pblk(D_MODEL, P_G), pblk(D_MODEL, P_G + D_MODEL), pblk(D_MODEL, P_G + 2 * D_MODEL),
                  row(D_MODEL)] + [_layer_spec(a, layer) for a in params],
        out_specs=row(D_MODEL),
        out_shape=jax.ShapeDtypeStruct((tokens, D_MODEL), F32),
        compiler_params=pltpu.CompilerParams(
            dimension_semantics=("arbitrary",), vmem_limit_bytes=VMEM_LIMIT_BYTES),
        name="merge",
    )(p, p, yb, yc, p, p, p, x2d, *params)


def _ffn_kernel(x_ref, g_ref, wu_ref, cw_ref, cb_ref, wd_ref, o_ref, h_scr, cbuf_scr, carry_scr,
                *, tm, fc, tiles_per_seq):
    i = pl.program_id(0)
    first = (i % tiles_per_seq) == 0
    nrb = tm // ROW_BLOCK
    nc = FFN_DIM // fc
    rows = [slice(rb * ROW_BLOCK, (rb + 1) * ROW_BLOCK) for rb in range(nrb)]

    def norm_rows(rb):
        x = x_ref[rows[rb], :]
        h = x * lax.rsqrt(jnp.mean(x * x, axis=-1, keepdims=True) + EPS) * g_ref[...]
        h_scr[rows[rb], :] = h.astype(BF16)

    def cols(half, c):
        return slice(half * FFN_DIM + c * fc, half * FFN_DIM + (c + 1) * fc)

    def up(rb, c):
        if c == 0:
            norm_rows(rb)
        hb = h_scr[rows[rb], :]
        return tuple(_dot(hb, wu_ref[:, cols(half, c)]) for half in range(2))

    def conv(rb, raw, cs):
        r0 = SUBLANE + rb * ROW_BLOCK
        if rb == 0:
            cbuf_scr[0:SUBLANE, cs] = jnp.where(first, 0.0, carry_scr[:, cs])
        cbuf_scr[r0:r0 + ROW_BLOCK, cs] = raw
        if rb == nrb - 1:
            carry_scr[:, cs] = raw[ROW_BLOCK - SUBLANE:, :]
        ext = cbuf_scr[r0 - SUBLANE:r0 + ROW_BLOCK, cs]
        y = cb_ref[:, cs] + cw_ref[FFN_CONV - 1:FFN_CONV, cs] * raw
        for k in range(FFN_CONV - 1):
            shifted = pltpu.roll(ext, FFN_CONV - 1 - k, 0)[SUBLANE:, :]
            y = y + cw_ref[k:k + 1, cs] * shifted
        return y

    units = [(rb, c) for rb in range(nrb) for c in range(nc)]
    acc = [None]

    def consume(unit, raw):
        rb, c = unit
        gate = conv(rb, raw[0], cols(0, c))
        val = conv(rb, raw[1], cols(1, c))
        act = (_silu(gate) * val).astype(BF16)
        part = _dot(act, wd_ref[c * fc:(c + 1) * fc, :])
        acc[0] = part if c == 0 else acc[0] + part
        if c == nc - 1:
            o_ref[rows[rb], :] = x_ref[rows[rb], :] + acc[0]

    _pipelined(units, up, consume, depth=2)


def _ffn(x2d, seq, prm, layer, *, tm, fc):
    tokens = x2d.shape[0]
    resident = ("ffn_w_up", "ffn_w_down")
    names = ("ffn_norm_g", "ffn_w_up", "ffn_conv_w", "ffn_conv_b", "ffn_w_down")
    params = [prm[k] for k in names]
    return pl.pallas_call(
        functools.partial(_ffn_kernel, tm=tm, fc=fc, tiles_per_seq=seq // tm),
        grid=(tokens // tm,),
        in_specs=[pl.BlockSpec((tm, D_MODEL), lambda i: (i, 0))] + [
            _layer_spec(prm[k], layer, pipeline_mode=pl.Buffered(1)) if k in resident else _layer_spec(prm[k], layer)
            for k in names],
        out_specs=pl.BlockSpec((tm, D_MODEL), lambda i: (i, 0)),
        out_shape=jax.ShapeDtypeStruct((tokens, D_MODEL), F32),
        scratch_shapes=[
            pltpu.VMEM((tm, D_MODEL), BF16),
            pltpu.VMEM((tm + SUBLANE, 2 * FFN_DIM), F32),
            pltpu.VMEM((SUBLANE, 2 * FFN_DIM), F32),
        ],
        compiler_params=pltpu.CompilerParams(
            dimension_semantics=("arbitrary",), vmem_limit_bytes=VMEM_LIMIT_BYTES),
        name="ffn",
    )(x2d, *params)


def _tile(seq, want):
    t = min(want, seq)
    assert seq % t == 0, (seq, t)
    return t


def _prepare(w):
    depth = w["w_in"].shape[0]
    head_pad = ((0, 0), (0, LANE - SSM_HEADS))
    row = lambda a: a.reshape(depth, 1, a.shape[-1])
    w_in = w["w_in"].astype(BF16)
    causal = np.tril(np.ones((GM_CHUNK, GM_CHUNK), bool))
    bound = (1.02 * LOG2E * math.sqrt(DA_HEAD_DIM) * jnp.max(jnp.abs(w["da_q_norm_g"]), axis=-1)
             * jnp.max(jnp.abs(w["da_k_norm_g"]), axis=-1))
    return {
        "attn_norm_g": row(w["attn_norm_g"]),
        "w_main": jnp.concatenate([w_in[:, :, :DT_OFF], w_in[:, :, DT_OFF + SSM_HEADS:]], axis=2),
        "w_dt": jnp.pad(w_in[:, :, DT_OFF:DT_OFF + SSM_HEADS], ((0, 0),) + head_pad),
        "dt_bias": row(jnp.pad(w["ssm_dt_bias"], head_pad)),
        "gm_v_norm_g": row(w["gm_v_norm_g"]),
        "ssm_conv_w": jnp.swapaxes(w["ssm_conv_w"], 1, 2),
        "ssm_conv_b": row(w["ssm_conv_b"]),
        "da_q_norm_g": row(jnp.tile(w["da_q_norm_g"], (1, LANE // DA_HEAD_DIM))),
        "da_k_norm_g": row(jnp.tile(w["da_k_norm_g"], (1, LANE // DA_HEAD_DIM))),
        "ssm_a_log": row(jnp.pad(w["ssm_a_log"], head_pad)),
        "ssm_d": row(jnp.repeat(w["ssm_d"], SSM_HEAD_DIM, axis=1)),
        "ssm_norm_g": row(w["ssm_norm_g"]),
        "attn_shift": jnp.stack([bound, (bound <= MAX_STATIC_SHIFT).astype(F32)], axis=1).astype(F32),
        "da_lambda": w["da_lambda"],
        "da_subln_g": row(w["da_subln_g"]),
        "gm_w_s": jnp.where(causal, w["gm_w_s"], 0.0).astype(BF16),
        "gm_b_s": jnp.repeat(jnp.swapaxes(w["gm_b_s"], 1, 2), GM_WIDTH // GM_GROUPS, axis=2),
        "w_branch_a": w["w_branch_a"].astype(BF16),
        "w_branch_b": w["w_branch_b"].astype(BF16),
        "w_branch_c": w["w_branch_c"].astype(BF16),
        "w_out": w["w_out"].astype(BF16),
        "ffn_norm_g": row(w["ffn_norm_g"]),
        "ffn_w_up": w["ffn_w_up"].astype(BF16),
        "ffn_conv_w": jnp.swapaxes(w["ffn_conv_w"], 1, 2),
        "ffn_conv_b": row(w["ffn_conv_b"]),
        "ffn_w_down": w["ffn_w_down"].astype(BF16),
    }


def kernel(x, attn_norm_g, w_in, gm_v_norm_g, gm_w_s, gm_b_s, ssm_conv_w, ssm_conv_b, ssm_dt_bias, ssm_a_log, ssm_d, ssm_norm_g, da_q_norm_g, da_k_norm_g, da_lambda, da_subln_g, w_branch_a, w_branch_b, w_branch_c, w_out, ffn_norm_g, ffn_w_up, ffn_conv_w, ffn_conv_b, ffn_w_down):
    batch, seq, _ = x.shape
    depth = w_in.shape[0]
    tokens = batch * seq
    tq = _tile(seq, 1024)
    tm_proj = _tile(seq, 512)
    tm_merge = _tile(seq, 512)
    tm_ffn = _tile(seq, 512)
    prm = _prepare(dict(
        attn_norm_g=attn_norm_g, w_in=w_in, gm_v_norm_g=gm_v_norm_g, gm_w_s=gm_w_s, gm_b_s=gm_b_s,
        ssm_conv_w=ssm_conv_w, ssm_conv_b=ssm_conv_b, ssm_dt_bias=ssm_dt_bias, ssm_a_log=ssm_a_log, ssm_d=ssm_d,
        ssm_norm_g=ssm_norm_g, da_q_norm_g=da_q_norm_g, da_k_norm_g=da_k_norm_g, da_lambda=da_lambda,
        da_subln_g=da_subln_g, w_branch_a=w_branch_a, w_branch_b=w_branch_b, w_branch_c=w_branch_c, w_out=w_out,
        ffn_norm_g=ffn_norm_g, ffn_w_up=ffn_w_up, ffn_conv_w=ffn_conv_w, ffn_conv_b=ffn_conv_b,
        ffn_w_down=ffn_w_down))
    rope_tabs = _rope_tables(seq)
    r = np.arange(LANE)
    bd = jnp.asarray((r[:, None] // DA_HEAD_DIM) == (r[None, :] // DA_HEAD_DIM), BF16)
    ssd_consts = _ssd_consts()
    x2d = x.reshape(tokens, D_MODEL)
    for layer in range(depth):
        lam_init = 0.8 - 0.6 * math.exp(-0.3 * layer)
        p, dt, vt = _in_proj(x2d, seq, prm, layer, rope_tabs, bd, tm=tm_proj, tk=tq)
        yb = _ssd(p, dt, batch, seq, prm, layer, ssd_consts, n_chunks=4)
        yc = _diff_attn(p, vt, batch, seq, prm, layer, lam_init, tq=tq)
        x2d = _merge(p, yb, yc, x2d, prm, layer, tm=tm_merge)
        x2d = _ffn(x2d, seq, prm, layer, tm=tm_ffn, fc=256)
    return x2d.reshape(batch, seq, D_MODEL)
```
